```python
import numpy as np
import jax
import jax.numpy as jnp
from jax import lax

D_MODEL = 1024
BATCH = 2
SEQ = 16384
DEPTH = 4
DEC_BATCH = 1
DEC_SEQ = 16384
PAST_LEN = 128

HEAD_DIM = 64
N_HEADS = D_MODEL // HEAD_DIM
N_HEADS_DIL = N_HEADS // 2
N_HEADS_NA = N_HEADS - N_HEADS_DIL
W_DIL = N_HEADS_DIL * HEAD_DIM
W_NA = N_HEADS_NA * HEAD_DIM
DIL_PAIRS = ((128, 1), (512, 4), (2048, 16))
ROT_DIM = HEAD_DIM // 4
ROPE_THETA = 500000.0
GRID_W = 64
NB_ROWS = 8
NB_COLS = 16
NA_QCOLS = 16
NA_KCOLS = NA_QCOLS + NB_COLS
N_GROUPS = 4
EXPERTS_PER_GROUP = 8
N_EXPERTS = N_GROUPS * EXPERTS_PER_GROUP
TOP_K_INNER = 2
D_EXPERT = D_MODEL // 2
MOE_BLOCK = 256
EPS = 1e-6
NEG = -1e30

kernel_name = "hybrid_dilated_neighbourhood_hmoe_encoder"


def rms_norm(x, g):
    xf = x.astype(jnp.float32)
    y = xf * lax.rsqrt(jnp.mean(xf * xf, axis=-1, keepdims=True) + EPS)
    return (y * g.astype(jnp.float32)).astype(x.dtype)


def partial_rope(x, pos):
    half = ROT_DIM // 2
    inv = ROPE_THETA ** (-jnp.arange(half, dtype=jnp.float32) / half)
    ang = pos.astype(jnp.float32)[:, None] * inv[None, :]
    cos = jnp.cos(ang)[None, :, None, :]
    sin = jnp.sin(ang)[None, :, None, :]
    xr = x[..., :ROT_DIM].astype(jnp.float32)
    x1, x2 = xr[..., :half], xr[..., half:]
    rot = jnp.concatenate([x1 * cos - x2 * sin, x2 * cos + x1 * sin], axis=-1).astype(x.dtype)
    return jnp.concatenate([rot, x[..., ROT_DIM:]], axis=-1)


def strided_window_attn(q, k, v, window, dilation):
    B, S, H, D = q.shape
    n_side = (window // 2) // dilation
    L = S // dilation
    blk = n_side
    nb = -(-L // blk)
    lp = nb * blk

    def to_classes(t):
        return t.reshape(B, L, dilation, H, D).transpose(0, 2, 1, 3, 4).reshape(B * dilation, L, H, D)

    qc, kc, vc = to_classes(q), to_classes(k), to_classes(v)
    qb = jnp.pad(qc, ((0, 0), (0, lp - L), (0, 0), (0, 0))).reshape(-1, nb, blk, H, D)
    pad_kv = ((0, 0), (blk, lp - L + blk), (0, 0), (0, 0))

    def key_blocks(t):
        t = jnp.pad(t, pad_kv).reshape(-1, nb + 2, blk, H, D)
        return jnp.concatenate([t[:, :-2], t[:, 1:-1], t[:, 2:]], axis=2)

    kb, vb = key_blocks(kc), key_blocks(vc)
    qi = jnp.arange(blk)[:, None]
    kj = jnp.arange(3 * blk)[None, :]
    rel = kj - blk - qi
    key_pos = jnp.arange(nb)[:, None, None] * blk + kj[None] - blk
    mask = (jnp.abs(rel)[None] <= n_side) & (key_pos >= 0) & (key_pos < L)
    s = jnp.einsum('nbqhd,nbkhd->nbhqk', qb, kb).astype(jnp.float32) * (D ** -0.5)
    s = jnp.where(mask[None, :, None], s, NEG)
    m = jnp.max(s, axis=-1, keepdims=True)
    p = jnp.exp(s - m)
    den = jnp.sum(p, axis=-1, keepdims=True)
    o = jnp.einsum('nbhqk,nbkhd->nbqhd', (p / den).astype(v.dtype), vb)
    lse = (m + jnp.log(den))[..., 0].transpose(0, 1, 3, 2)
    o = o.reshape(B, dilation, lp, H, D)[:, :, :L].transpose(0, 2, 1, 3, 4).reshape(B, S, H, D)
    lse = lse.reshape(B, dilation, lp, H)[:, :, :L].transpose(0, 2, 1, 3).reshape(B, S, H)
    return o, lse


def dilated_mixture_attn(q, k, v):
    outs, lses = [], []
    for window, dilation in DIL_PAIRS:
        o, l = strided_window_attn(q, k, v, window, dilation)
        outs.append(o)
        lses.append(l)
    wts = jax.nn.softmax(jnp.stack(lses), axis=0)
    o = jnp.einsum('pbsh,pbshd->bshd', wts, jnp.stack(outs).astype(jnp.float32))
    return o.astype(q.dtype)


def neighbourhood_attn(q, k, v, rpb):
    B, S, H, D = q.shape
    rows = S // GRID_W
    kr = min(NB_ROWS, rows)
    ncb = GRID_W // NA_QCOLS
    qcol = np.arange(GRID_W).reshape(ncb, NA_QCOLS)
    span0 = np.clip(np.arange(ncb) * NA_QCOLS - NB_COLS // 2, 0, GRID_W - NA_KCOLS)
    kcol = span0[:, None] + np.arange(NA_KCOLS)
    wstart = np.clip(qcol - NB_COLS // 2, 0, GRID_W - NB_COLS)
    col_ok = (kcol[:, None, :] >= wstart[..., None]) & (kcol[:, None, :] < wstart[..., None] + NB_COLS)
    mask = np.broadcast_to(col_ok[:, :, None, :], (ncb, NA_QCOLS, kr, NA_KCOLS)).reshape(ncb, 1, NA_QCOLS, kr * NA_KCOLS)
    mask = jnp.asarray(mask)
    col_idx = np.clip(kcol[:, None, :] - qcol[:, :, None] + NB_COLS - 1, 0, 2 * NB_COLS - 2)
    rpb_cols = rpb.astype(jnp.float32)[:, :, col_idx]
    qg = q.reshape(B, rows, ncb, NA_QCOLS, H, D)
    kg = k.reshape(B, rows, GRID_W, H, D)
    vg = v.reshape(B, rows, GRID_W, H, D)
    scale = D ** -0.5

    def one_row(r):
        r0 = jnp.clip(r - kr // 2, 0, rows - kr)
        k_rows = lax.dynamic_slice_in_dim(kg, r0, kr, axis=1)
        v_rows = lax.dynamic_slice_in_dim(vg, r0, kr, axis=1)
        kb = k_rows[:, :, kcol].transpose(0, 2, 1, 3, 4, 5).reshape(B, ncb, kr * NA_KCOLS, H, D)
        vb = v_rows[:, :, kcol].transpose(0, 2, 1, 3, 4, 5).reshape(B, ncb, kr * NA_KCOLS, H, D)
        q_r = lax.dynamic_index_in_dim(qg, r, axis=1, keepdims=False)
        s = jnp.einsum('bcqhd,bckhd->bchqk', q_r, kb).astype(jnp.float32) * scale
        drow = r0 + jnp.arange(kr) - r + NB_ROWS - 1
        bias = rpb_cols[:, drow].transpose(2, 0, 3, 1, 4).reshape(ncb, H, NA_QCOLS, kr * NA_KCOLS)
        s = jnp.where(mask, s + bias, NEG)
        p = jax.nn.softmax(s, axis=-1)
        o = jnp.einsum('bchqk,bckhd->bcqhd', p.astype(v.dtype), vb)
        return o.reshape(B, GRID_W, H, D)

    out = lax.map(one_row, jnp.arange(rows))
    return out.transpose(1, 0, 2, 3, 4).reshape(B, S, H, D)


def mixer_block(x, ln_g, w_in, qn_d, kn_d, qn_n, kn_n, rpb, on_d, on_n, w_out):
    B, S, _ = x.shape
    h = rms_norm(x, ln_g)
    proj = h @ w_in
    o1, o2, o3 = W_DIL, 2 * W_DIL, 3 * W_DIL
    q_d, k_d, v_d = proj[..., :o1], proj[..., o1:o2], proj[..., o2:o3]
    q_n, k_n, v_n = proj[..., o3:o3 + W_NA], proj[..., o3 + W_NA:o3 + 2 * W_NA], proj[..., o3 + 2 * W_NA:]
    pos = jnp.arange(S)
    hd = lambda t, n: t.reshape(B, S, n, HEAD_DIM)
    q_d = partial_rope(rms_norm(hd(q_d, N_HEADS_DIL), qn_d), pos)
    k_d = partial_rope(rms_norm(hd(k_d, N_HEADS_DIL), kn_d), pos)
    y_d = dilated_mixture_attn(q_d, k_d, hd(v_d, N_HEADS_DIL)).reshape(B, S, W_DIL)
    q_n = rms_norm(hd(q_n, N_HEADS_NA), qn_n)
    k_n = rms_norm(hd(k_n, N_HEADS_NA), kn_n)
    y_n = neighbourhood_attn(q_n, k_n, hd(v_n, N_HEADS_NA), rpb).reshape(B, S, W_NA)
    y = jnp.concatenate([rms_norm(y_d, on_d), rms_norm(y_n, on_n)], axis=-1)
    return x + y @ w_out


def routed_experts(h, eid, gate, w_gate, w_up, w_down):
    T, Dm = h.shape
    A = eid.shape[0]
    tok = jnp.arange(A) // TOP_K_INNER
    order = jnp.argsort(eid)
    e_sorted = eid[order]
    counts = jnp.zeros((N_EXPERTS,), jnp.int32).at[eid].add(1)
    padded = (counts + MOE_BLOCK - 1) // MOE_BLOCK * MOE_BLOCK
    start = jnp.cumsum(counts) - counts
    pend = jnp.cumsum(padded)
    pstart = pend - padded
    dest = pstart[e_sorted] + jnp.arange(A) - start[e_sorted]
    n_slots = (-(-A // MOE_BLOCK) + N_EXPERTS) * MOE_BLOCK
    n_blocks = n_slots // MOE_BLOCK
    slot_tok = jnp.full((n_slots,), T, jnp.int32).at[dest].set(tok[order])
    slot_gate = jnp.zeros((n_slots,), jnp.float32).at[dest].set(gate[order])
    block_e = jnp.minimum(jnp.sum(jnp.arange(n_blocks)[:, None] * MOE_BLOCK >= pend[None, :], axis=-1), N_EXPERTS - 1)
    h_pad = jnp.concatenate([h, jnp.zeros((1, Dm), h.dtype)], axis=0)
    xb = h_pad[slot_tok].reshape(n_blocks, MOE_BLOCK, Dm)

    def expert_block(args):
        xe, e = args
        return (jax.nn.silu(xe @ w_gate[e]) * (xe @ w_up[e])) @ w_down[e]

    yb = lax.map(expert_block, (xb, block_e)).reshape(n_slots, Dm)
    yb = yb * slot_gate[:, None].astype(yb.dtype)
    return jax.ops.segment_sum(yb, slot_tok, num_segments=T + 1)[:T]


def hier_moe_block(x, ln_g, w_rg, b_rg, w_re, b_re, w_gate, w_up, w_down):
    B, S, Dm = x.shape
    h = rms_norm(x, ln_g).reshape(B * S, Dm)
    hf = h.astype(jnp.float32)
    p_group = jax.nn.softmax(hf @ w_rg.astype(jnp.float32) + b_rg.astype(jnp.float32), axis=-1)
    g_sel = jnp.argmax(p_group, axis=-1)
    g_gate = jnp.max(p_group, axis=-1)
    inner = jnp.einsum('td,gde->tge', hf, w_re.astype(jnp.float32)) + b_re.astype(jnp.float32)
    sel = jnp.take_along_axis(inner, g_sel[:, None, None], axis=1)[:, 0]
    top_v, top_i = lax.top_k(sel, TOP_K_INNER)
    gates = g_gate[:, None] * jax.nn.softmax(top_v, axis=-1)
    eid = (g_sel[:, None] * EXPERTS_PER_GROUP + top_i).reshape(-1).astype(jnp.int32)
    y = routed_experts(h, eid, gates.reshape(-1), w_gate, w_up, w_down)
    return x + y.reshape(B, S, Dm)


def trunk(x, ln_mix, w_in, q_norm_dil, k_norm_dil, q_norm_na, k_norm_na, rpb_na, out_norm_dil, out_norm_na,
          w_out, ln_ffn, w_router_group, b_router_group, w_router_expert, b_router_expert, w_gate, w_up, w_down):
    for l in range(DEPTH):
        x = mixer_block(x, ln_mix[l], w_in[l], q_norm_dil[l], k_norm_dil[l], q_norm_na[l], k_norm_na[l],
                        rpb_na[l], out_norm_dil[l], out_norm_na[l], w_out[l])
        x = hier_moe_block(x, ln_ffn[l], w_router_group[l], b_router_group[l], w_router_expert[l],
                           b_router_expert[l], w_gate[l], w_up[l], w_down[l])
    return x


def setup_inputs(seed: int = 0) -> dict:
    key = jax.random.key(seed)
    ks = jax.random.split(key, 20)
    f32 = jnp.float32
    nrm = lambda k, shape, s: jax.random.normal(k, shape, f32) * s
    gain = lambda k, shape: 1.0 + 0.02 * jax.random.normal(k, shape, f32)
    return {
        "x_prompt": nrm(ks[0], (BATCH, SEQ, D_MODEL), 1.0),
        "x_sample": nrm(ks[1], (DEC_BATCH, DEC_SEQ, D_MODEL), 1.0),
        "ln_mix": gain(ks[2], (DEPTH, D_MODEL)),
        "w_in": nrm(ks[3], (DEPTH, D_MODEL, 3 * D_MODEL), D_MODEL ** -0.5),
        "q_norm_dil": gain(ks[4], (DEPTH, HEAD_DIM)),
        "k_norm_dil": gain(ks[5], (DEPTH, HEAD_DIM)),
        "q_norm_na": gain(ks[6], (DEPTH, HEAD_DIM)),
        "k_norm_na": gain(ks[7], (DEPTH, HEAD_DIM)),
        "rpb_na": nrm(ks[8], (DEPTH, N_HEADS_NA, 2 * NB_ROWS - 1, 2 * NB_COLS - 1), 0.1),
        "out_norm_dil": gain(ks[9], (DEPTH, W_DIL)),
        "out_norm_na": gain(ks[10], (DEPTH, W_NA)),
        "w_out": nrm(ks[11], (DEPTH, D_MODEL, D_MODEL), D_MODEL ** -0.5),
        "ln_ffn": gain(ks[12], (DEPTH, D_MODEL)),
        "w_router_group": nrm(ks[13], (DEPTH, D_MODEL, N_GROUPS), D_MODEL ** -0.5),
        "b_router_group": nrm(ks[14], (DEPTH, N_GROUPS), 0.01),
        "w_router_expert": nrm(ks[15], (DEPTH, N_GROUPS, D_MODEL, EXPERTS_PER_GROUP), D_MODEL ** -0.5),
        "b_router_expert": nrm(ks[16], (DEPTH, N_GROUPS, EXPERTS_PER_GROUP), 0.01),
        "w_gate": nrm(ks[17], (DEPTH, N_EXPERTS, D_MODEL, D_EXPERT), D_MODEL ** -0.5),
        "w_up": nrm(ks[18], (DEPTH, N_EXPERTS, D_MODEL, D_EXPERT), D_MODEL ** -0.5),
        "w_down": nrm(ks[19], (DEPTH, N_EXPERTS, D_EXPERT, D_MODEL), D_EXPERT ** -0.5),
    }


def reference(x_prompt, x_sample, ln_mix, w_in, q_norm_dil, k_norm_dil, q_norm_na, k_norm_na, rpb_na,
              out_norm_dil, out_norm_na, w_out, ln_ffn, w_router_group, b_router_group, w_router_expert,
              b_router_expert, w_gate, w_up, w_down):
    y_prompt = trunk(x_prompt, ln_mix, w_in, q_norm_dil, k_norm_dil, q_norm_na, k_norm_na, rpb_na,
                     out_norm_dil, out_norm_na, w_out, ln_ffn, w_router_group, b_router_group,
                     w_router_expert, b_router_expert, w_gate, w_up, w_down)
    y_sample = trunk(x_sample, ln_mix, w_in, q_norm_dil, k_norm_dil, q_norm_na, k_norm_na, rpb_na,
                     out_norm_dil, out_norm_na, w_out, ln_ffn, w_router_group, b_router_group,
                     w_router_expert, b_router_expert, w_gate, w_up, w_down)
    return (y_prompt, y_sample)
```

```python
import functools

import numpy as np
import jax
import jax.numpy as jnp
from jax import lax
from jax.experimental import pallas as pl
from jax.experimental.pallas import tpu as pltpu

D_MODEL = 1024
DEPTH = 4
HEAD_DIM = 64
N_HEADS_DIL = 8
N_HEADS_NA = 8
W_DIL = N_HEADS_DIL * HEAD_DIM
W_NA = N_HEADS_NA * HEAD_DIM
DILATIONS = (1, 4, 16)
BAND_SIDE = 64
ROT_DIM = HEAD_DIM // 4
ROPE_THETA = 500000.0
GRID_W = 64
NB_ROWS = 8
NB_COLS = 16
N_GROUPS = 4
EXPERTS_PER_GROUP = 8
N_EXPERTS = N_GROUPS * EXPERTS_PER_GROUP
TOP_K = 2
D_EXPERT = D_MODEL // 2
EPS = 1e-6
NEG = -1e30

LANES = 128
MXU_DTYPE = jnp.bfloat16
VMEM_LIMIT = 48 * 1024 * 1024

TM_PROJ = 512
TQ_BAND = 512
SUB_BAND = 128
NA_ROWS_STEP = 8
TM_ROUTE = 512
MOE_BLOCK = 512


def _cparams(*sem):
    return pltpu.CompilerParams(dimension_semantics=sem, vmem_limit_bytes=VMEM_LIMIT)


def _in_proj_kernel(l_ref, x_ref, g_ref, w_ref, hg_ref, bd_ref, rope_ref, o_ref):
    del l_ref
    x = x_ref[...]
    ms = jnp.mean(x * x, axis=-1, keepdims=True)
    h = (x * lax.rsqrt(ms + EPS) * g_ref[...]).astype(MXU_DTYPE)
    cos = rope_ref[:, 0:LANES]
    sin_up = rope_ref[:, LANES:2 * LANES]
    sin_dn = rope_ref[:, 2 * LANES:3 * LANES]
    half = ROT_DIM // 2
    for c in range(6):
        acc = jnp.dot(h, w_ref[:, c * W_DIL:(c + 1) * W_DIL], preferred_element_type=jnp.float32)
        if c in (0, 1, 3, 4):
            gi = (0, 1, None, 2, 3)[c]
            msh = jnp.dot((acc * acc).astype(MXU_DTYPE), bd_ref[...], preferred_element_type=jnp.float32)
            acc = acc * lax.rsqrt(msh + EPS) * hg_ref[gi:gi + 1, :]
        if c in (0, 1):
            parts = []
            for j in range(W_DIL // LANES):
                t = acc[:, j * LANES:(j + 1) * LANES]
                parts.append(t * cos + pltpu.roll(t, LANES - half, 1) * sin_up + pltpu.roll(t, half, 1) * sin_dn)
            acc = jnp.concatenate(parts, axis=1)
        if c in (0, 3):
            acc = acc * (HEAD_DIM ** -0.5)
        o_ref[:, c * W_DIL:(c + 1) * W_DIL] = acc.astype(o_ref.dtype)


def _in_proj(x2d, lidx, ln_g, w_in, head_gains, blockdiag, rope_tab, seq):
    t, d = x2d.shape
    tm = min(TM_PROJ, seq)
    n_seq_blocks = seq // tm
    grid_spec = pltpu.PrefetchScalarGridSpec(
        num_scalar_prefetch=1,
        grid=(t // tm,),
        in_specs=[
            pl.BlockSpec((tm, d), lambda i, l: (i, 0)),
            pl.BlockSpec((None, 1, d), lambda i, l: (l[0], 0, 0)),
            pl.BlockSpec((None, d, 3 * d), lambda i, l: (l[0], 0, 0)),
            pl.BlockSpec((None, 4, W_DIL), lambda i, l: (l[0], 0, 0)),
            pl.BlockSpec((W_DIL, W_DIL), lambda i, l: (0, 0)),
            pl.BlockSpec((tm, 3 * LANES), lambda i, l: (i % n_seq_blocks, 0)),
        ],
        out_specs=pl.BlockSpec((tm, 3 * d), lambda i, l: (i, 0)),
    )
    return pl.pallas_call(
        _in_proj_kernel,
        grid_spec=grid_spec,
        out_shape=jax.ShapeDtypeStruct((t, 3 * d), MXU_DTYPE),
        compiler_params=_cparams("parallel"),
        name="in_proj",
    )(lidx, x2d, ln_g, w_in, head_gains, blockdiag, rope_tab)


def _pair_rows(x, low_mask):
    zero = jnp.zeros_like(x)
    return jnp.concatenate([jnp.where(low_mask, x, zero), jnp.where(low_mask, zero, x)], axis=0)


def _softmax_pv(s, v):
    m = jnp.max(s, axis=-1, keepdims=True)
    p = jnp.exp(s - m)
    den = jnp.sum(p, axis=-1, keepdims=True)
    o = jnp.dot(p.astype(MXU_DTYPE), v, preferred_element_type=jnp.float32)
    return o * (1.0 / den), m + jnp.log(den)


def _band_kernel(*refs, tq, n_l, has_prev, is_last):
    q_ref, kc_ref, kp_ref, kn_ref, vc_ref, vp_ref, vn_ref, band_ref = refs[:8]
    pos = 8
    if has_prev:
        po_ref, pl_ref = refs[pos:pos + 2]
        pos += 2
    o_ref = refs[pos]
    pos += 1
    if not is_last:
        lse_ref = refs[pos]
        pos += 1
    kk_ref, vv_ref = refs[pos:pos + 2]

    li = pl.program_id(2)
    side = BAND_SIDE
    kk_ref[0:side, :] = kp_ref[...]
    kk_ref[side:side + tq, :] = kc_ref[...]
    kk_ref[side + tq:, :] = kn_ref[...]
    vv_ref[0:side, :] = vp_ref[...]
    vv_ref[side:side + tq, :] = vc_ref[...]
    vv_ref[side + tq:, :] = vn_ref[...]

    sub = SUB_BAND
    nk = sub + 2 * side
    low = lax.broadcasted_iota(jnp.int32, (1, LANES), 1) < HEAD_DIM
    col = lax.broadcasted_iota(jnp.int32, (1, nk), 1)
    n_sub = tq // sub
    for hp in range(W_DIL // LANES):
        cs = slice(hp * LANES, (hp + 1) * LANES)
        for j in range(n_sub):
            q2 = _pair_rows(q_ref[j * sub:(j + 1) * sub, cs], low)
            keys = kk_ref[j * sub:j * sub + nk, cs]
            s = lax.dot_general(q2, keys, (((1,), (1,)), ((), ())), preferred_element_type=jnp.float32)
            s = s + band_ref[...]
            if j == 0:
                s = jnp.where((col >= side) | (li > 0), s, NEG)
            if j == n_sub - 1:
                s = jnp.where((col < nk - side) | (li < n_l - 1), s, NEG)
            o2, lse2 = _softmax_pv(s, vv_ref[j * sub:j * sub + nk, cs])
            o = jnp.where(low, o2[:sub], o2[sub:])
            lse = jnp.where(low, lse2[:sub], lse2[sub:])
            rows = slice(j * sub, (j + 1) * sub)
            if has_prev:
                o_a = po_ref[rows, cs]
                lse_a = pl_ref[rows, cs]
                mx = jnp.maximum(lse_a, lse)
                w_a = jnp.exp(lse_a - mx)
                w_b = jnp.exp(lse - mx)
                tot = w_a + w_b
                o = (w_a * o_a + w_b * o) * (1.0 / tot)
                lse = mx + jnp.log(tot)
            o_ref[rows, cs] = o
            if not is_last:
                lse_ref[rows, cs] = lse


def _band_attn(proj, dil, band_bias, prev, is_last):
    b, seq, width = proj.shape
    n_col = width // W_DIL
    l_len = seq // dil
    tq = min(TQ_BAND, l_len)
    n_l = l_len // tq
    halo_per_tile = tq // BAND_SIDE
    n_halo = l_len // BAND_SIDE
    pv = proj.reshape(b, l_len, dil * width)

    def cur(which):
        return pl.BlockSpec((None, tq, W_DIL), lambda bi, r, l: (bi, l, r * n_col + which))

    def before(which):
        return pl.BlockSpec((None, BAND_SIDE, W_DIL),
                            lambda bi, r, l: (bi, jnp.maximum(l * halo_per_tile - 1, 0), r * n_col + which))

    def after(which):
        return pl.BlockSpec((None, BAND_SIDE, W_DIL),
                            lambda bi, r, l: (bi, jnp.minimum((l + 1) * halo_per_tile, n_halo - 1), r * n_col + which))

    out_spec = pl.BlockSpec((None, tq, W_DIL), lambda bi, r, l: (bi, l, r))
    nk = SUB_BAND + 2 * BAND_SIDE
    in_specs = [cur(0), cur(1), before(1), after(1), cur(2), before(2), after(2),
                pl.BlockSpec((2 * SUB_BAND, nk), lambda bi, r, l: (0, 0))]
    args = [pv, pv, pv, pv, pv, pv, pv, band_bias]
    has_prev = prev is not None
    if has_prev:
        in_specs += [out_spec, out_spec]
        args += [prev[0].reshape(b, l_len, dil * W_DIL), prev[1].reshape(b, l_len, dil * W_DIL)]
    o_shape = jax.ShapeDtypeStruct((b, l_len, dil * W_DIL), jnp.float32)
    out_shape = [o_shape] if is_last else [o_shape, o_shape]
    out_specs = [out_spec] if is_last else [out_spec, out_spec]
    outs = pl.pallas_call(
        functools.partial(_band_kernel, tq=tq, n_l=n_l, has_prev=has_prev, is_last=is_last),
        grid=(b, dil, n_l),
        in_specs=in_specs,
        out_specs=out_specs,
        out_shape=out_shape,
        scratch_shapes=[pltpu.VMEM((tq + 2 * BAND_SIDE, W_DIL), MXU_DTYPE),
                        pltpu.VMEM((tq + 2 * BAND_SIDE, W_DIL), MXU_DTYPE)],
        compiler_params=_cparams("parallel", "parallel", "parallel"),
        name=f"band_attn_d{dil}",
    )(*args)
    return [o.reshape(b, seq, W_DIL) for o in outs]


def _band_bias():
    nk = SUB_BAND + 2 * BAND_SIDE
    r = np.arange(2 * SUB_BAND)[:, None] % SUB_BAND
    c = np.arange(nk)[None, :]
    ok = (c - r >= 0) & (c - r <= 2 * BAND_SIDE)
    return jnp.asarray(np.where(ok, 0.0, NEG), jnp.float32)


def _na_kernel(q_ref, kc_ref, kp_ref, kn_ref, vc_ref, vp_ref, vn_ref, bias_ref, o_ref, kk_ref, vv_ref, *, n_rows):
    step = pl.program_id(1)
    halo = (NB_ROWS // 2) * GRID_W
    cur = NA_ROWS_STEP * GRID_W
    kk_ref[0:halo, :] = kp_ref[...]
    kk_ref[halo:halo + cur, :] = kc_ref[...]
    kk_ref[halo + cur:, :] = kn_ref[...]
    vv_ref[0:halo, :] = vp_ref[...]
    vv_ref[halo:halo + cur, :] = vc_ref[...]
    vv_ref[halo + cur:, :] = vn_ref[...]

    low = lax.broadcasted_iota(jnp.int32, (1, LANES), 1) < HEAD_DIM
    n_keys = NB_ROWS * GRID_W
    row_base = step * NA_ROWS_STEP

    def one_row(i, carry):
        r = row_base + i
        r0 = jnp.clip(r - NB_ROWS // 2, 0, n_rows - NB_ROWS)
        variant = r - r0
        start = pl.multiple_of((r0 - row_base + NB_ROWS // 2) * GRID_W, GRID_W)
        qrow = pl.multiple_of(i * GRID_W, GRID_W)
        for hp in range(W_NA // LANES):
            cs = slice(hp * LANES, (hp + 1) * LANES)
            q2 = _pair_rows(q_ref[pl.ds(qrow, GRID_W), cs], low)
            keys = kk_ref[pl.ds(start, n_keys), cs]
            s = lax.dot_general(q2, keys, (((1,), (1,)), ((), ())), preferred_element_type=jnp.float32)
            s = s + bias_ref[hp, variant]
            o2, _ = _softmax_pv(s, vv_ref[pl.ds(start, n_keys), cs])
            o_ref[pl.ds(qrow, GRID_W), cs] = jnp.where(low, o2[:GRID_W], o2[GRID_W:])
        return carry

    lax.fori_loop(0, NA_ROWS_STEP, one_row, 0)


def _na_attn(proj, bias_tab):
    b, seq, width = proj.shape
    n_rows = seq // GRID_W
    cur = NA_ROWS_STEP * GRID_W
    halo = (NB_ROWS // 2) * GRID_W
    halo_per_step = cur // halo
    n_halo = seq // halo
    n_pairs = W_NA // LANES

    def cur_spec(which):
        return pl.BlockSpec((None, cur, W_NA), lambda bi, i: (bi, i, which))

    def before(which):
        return pl.BlockSpec((None, halo, W_NA), lambda bi, i: (bi, jnp.maximum(i * halo_per_step - 1, 0), which))

    def after(which):
        return pl.BlockSpec((None, halo, W_NA),
                            lambda bi, i: (bi, jnp.minimum((i + 1) * halo_per_step, n_halo - 1), which))

    return pl.pallas_call(
        functools.partial(_na_kernel, n_rows=n_rows),
        grid=(b, n_rows // NA_ROWS_STEP),
        in_specs=[cur_spec(3), cur_spec(4), before(4), after(4), cur_spec(5), before(5), after(5),
                  pl.BlockSpec((n_pairs, NB_ROWS, 2 * GRID_W, NB_ROWS * GRID_W), lambda bi, i: (0, 0, 0, 0))],
        out_specs=pl.BlockSpec((None, cur, W_NA), lambda bi, i: (bi, i, 0)),
        out_shape=jax.ShapeDtypeStruct((b, seq, W_NA), jnp.float32),
        scratch_shapes=[pltpu.VMEM((cur + 2 * halo, W_NA), MXU_DTYPE),
                        pltpu.VMEM((cur + 2 * halo, W_NA), MXU_DTYPE)],
        compiler_params=_cparams("parallel", "parallel"),
        name="na_attn",
    )(proj, proj, proj, proj, proj, proj, proj, bias_tab)


def _na_bias_table(rpb):
    c = np.arange(GRID_W)
    wstart = np.clip(c - NB_COLS // 2, 0, GRID_W - NB_COLS)
    kc = np.arange(GRID_W)
    valid = (kc[None, :] >= wstart[:, None]) & (kc[None, :] < wstart[:, None] + NB_COLS)
    dc = np.clip(kc[None, :] - c[:, None] + NB_COLS - 1, 0, 2 * NB_COLS - 2)
    dr = np.arange(NB_ROWS)[None, :] - np.arange(NB_ROWS)[:, None] + NB_ROWS - 1
    tab = rpb.astype(jnp.float32)[:, dr][:, :, :, dc]
    tab = jnp.where(jnp.asarray(valid)[None, None, None], tab, NEG)
    tab = tab.transpose(0, 1, 3, 2, 4)
    n_pairs = N_HEADS_NA // 2
    tab = tab.reshape(n_pairs, 2, NB_ROWS, GRID_W, NB_ROWS * GRID_W).transpose(0, 2, 1, 3, 4)
    return tab.reshape(n_pairs, NB_ROWS, 2 * GRID_W, NB_ROWS * GRID_W)


def _out_proj_kernel(l_ref, yd_ref, yn_ref, x_ref, gd_ref, gn_ref, w_ref, o_ref):
    del l_ref

    def norm(y, g):
        ms = jnp.mean(y * y, axis=-1, keepdims=True)
        return (y * lax.rsqrt(ms + EPS) * g).astype(MXU_DTYPE)

    nd = norm(yd_ref[...], gd_ref[...])
    nn = norm(yn_ref[...], gn_ref[...])
    y = jnp.dot(nd, w_ref[0:W_DIL, :], preferred_element_type=jnp.float32)
    y = y + jnp.dot(nn, w_ref[W_DIL:, :], preferred_element_type=jnp.float32)
    o_ref[...] = x_ref[...] + y


def _out_proj(y_d, y_n, x2d, lidx, g_d, g_n, w_out, seq):
    t, d = x2d.shape
    tm = min(TM_PROJ, seq)
    grid_spec = pltpu.PrefetchScalarGridSpec(
        num_scalar_prefetch=1,
        grid=(t // tm,),
        in_specs=[
            pl.BlockSpec((tm, W_DIL), lambda i, l: (i, 0)),
            pl.BlockSpec((tm, W_NA), lambda i, l: (i, 0)),
            pl.BlockSpec((tm, d), lambda i, l: (i, 0)),
            pl.BlockSpec((None, 1, W_DIL), lambda i, l: (l[0], 0, 0)),
            pl.BlockSpec((None, 1, W_NA), lambda i, l: (l[0], 0, 0)),
            pl.BlockSpec((None, d, d), lambda i, l: (l[0], 0, 0)),
        ],
        out_specs=pl.BlockSpec((tm, d), lambda i, l: (i, 0)),
    )
    return pl.pallas_call(
        _out_proj_kernel,
        grid_spec=grid_spec,
        out_shape=jax.ShapeDtypeStruct((t, d), jnp.float32),
        compiler_params=_cparams("parallel"),
        name="out_proj",
    )(lidx, y_d, y_n, x2d, g_d, g_n, w_out)


def _router_kernel(l_ref, x_ref, g_ref, w_ref, b_ref, tri_ref, h_ref, eid_ref, rank_ref, gate_ref, cnt_ref, carry_ref):
    del l_ref

    @pl.when(pl.program_id(0) == 0)
    def _():
        carry_ref[...] = jnp.zeros_like(carry_ref)

    x = x_ref[...]
    ms = jnp.mean(x * x, axis=-1, keepdims=True)
    h = x * lax.rsqrt(ms + EPS) * g_ref[...]
    h_hi = h.astype(MXU_DTYPE)
    h_ref[...] = h_hi
    h_lo = (h - h_hi.astype(jnp.float32)).astype(MXU_DTYPE)
    both = jnp.dot(h_hi, w_ref[...], preferred_element_type=jnp.float32)
    logits = both[:, :LANES] + both[:, LANES:]
    logits = logits + jnp.dot(h_lo, w_ref[:, :LANES], preferred_element_type=jnp.float32) + b_ref[...]
    lt = logits.T
    tm = lt.shape[1]

    best = lt[N_EXPERTS:N_EXPERTS + 1]
    gsel = jnp.zeros_like(best)
    for g in range(1, N_GROUPS):
        cand = lt[N_EXPERTS + g:N_EXPERTS + g + 1]
        upd = cand > best
        gsel = jnp.where(upd, float(g), gsel)
        best = jnp.where(upd, cand, best)
    den = jnp.zeros_like(best)
    for g in range(N_GROUPS):
        den = den + jnp.exp(lt[N_EXPERTS + g:N_EXPERTS + g + 1] - best)
    g_gate = 1.0 / den

    e8 = EXPERTS_PER_GROUP
    sel = lt[0:e8]
    for g in range(1, N_GROUPS):
        sel = jnp.where(gsel == float(g), lt[g * e8:(g + 1) * e8], sel)
    row = lax.broadcasted_iota(jnp.int32, (e8, tm), 0).astype(jnp.float32)
    v1 = jnp.max(sel, axis=0, keepdims=True)
    i1 = jnp.min(jnp.where(sel == v1, row, float(e8)), axis=0, keepdims=True)
    rest = jnp.where(row == i1, -jnp.inf, sel)
    v2 = jnp.max(rest, axis=0, keepdims=True)
    i2 = jnp.min(jnp.where(rest == v2, row, float(e8)), axis=0, keepdims=True)
    e2 = jnp.exp(v2 - v1)
    inv = 1.0 / (1.0 + e2)
    gate_ref[0:1, :] = g_gate * inv
    gate_ref[1:2, :] = g_gate * (e2 * inv)
    eid1 = gsel * float(e8) + i1
    eid2 = gsel * float(e8) + i2
    eid_ref[0:1, :] = eid1.astype(jnp.int32)
    eid_ref[1:2, :] = eid2.astype(jnp.int32)

    erow = lax.broadcasted_iota(jnp.int32, (N_EXPERTS, tm), 0).astype(jnp.float32)
    hit1 = erow == eid1
    hit2 = erow == eid2
    oh = jnp.concatenate([hit1, hit2], axis=0).astype(jnp.float32)
    pref = jnp.dot(oh.astype(MXU_DTYPE), tri_ref[...], preferred_element_type=jnp.float32)
    tot = jnp.sum(oh, axis=1, keepdims=True)
    carry = carry_ref[:, 0:1]
    val1 = carry + pref[:N_EXPERTS]
    val2 = carry + tot[:N_EXPERTS] + pref[N_EXPERTS:]
    rank_ref[0:1, :] = jnp.sum(jnp.where(hit1, val1, 0.0), axis=0, keepdims=True).astype(jnp.int32)
    rank_ref[1:2, :] = jnp.sum(jnp.where(hit2, val2, 0.0), axis=0, keepdims=True).astype(jnp.int32)
    new_carry = carry_ref[...] + (tot[:N_EXPERTS] + tot[N_EXPERTS:])
    carry_ref[...] = new_carry
    cnt_ref[...] = new_carry


def _router(x2d, lidx, ln_g, w_route, b_route, tri):
    t, d = x2d.shape
    tm = min(TM_ROUTE, t)
    grid_spec = pltpu.PrefetchScalarGridSpec(
        num_scalar_prefetch=1,
        grid=(t // tm,),
        in_specs=[
            pl.BlockSpec((tm, d), lambda i, l: (i, 0)),
            pl.BlockSpec((None, 1, d), lambda i, l: (l[0], 0, 0)),
            pl.BlockSpec((None, d, 2 * LANES), lambda i, l: (l[0], 0, 0)),
            pl.BlockSpec((None, 1, LANES), lambda i, l: (l[0], 0, 0)),
            pl.BlockSpec((tm, tm), lambda i, l: (0, 0)),
        ],
        out_specs=[
            pl.BlockSpec((tm, d), lambda i, l: (i, 0)),
            pl.BlockSpec((TOP_K, tm), lambda i, l: (0, i)),
            pl.BlockSpec((TOP_K, tm), lambda i, l: (0, i)),
            pl.BlockSpec((TOP_K, tm), lambda i, l: (0, i)),
            pl.BlockSpec((N_EXPERTS, LANES), lambda i, l: (0, 0)),
        ],
        scratch_shapes=[pltpu.VMEM((N_EXPERTS, LANES), jnp.float32)],
    )
    return pl.pallas_call(
        _router_kernel,
        grid_spec=grid_spec,
        out_shape=[
            jax.ShapeDtypeStruct((t, d), MXU_DTYPE),
            jax.ShapeDtypeStruct((TOP_K, t), jnp.int32),
            jax.ShapeDtypeStruct((TOP_K, t), jnp.int32),
            jax.ShapeDtypeStruct((TOP_K, t), jnp.float32),
            jax.ShapeDtypeStruct((N_EXPERTS, LANES), jnp.float32),
        ],
        compiler_params=_cparams("arbitrary"),
        name="router",
    )(lidx, x2d, ln_g, w_route, b_route, tri)


def _expert_kernel(l_ref, be_ref, nu_ref, x_ref, wg_ref, wu_ref, wd_ref, o_ref):
    del l_ref, be_ref

    @pl.when(pl.program_id(0) < nu_ref[0])
    def _():
        x = x_ref[...]
        g = jnp.dot(x, wg_ref[...], preferred_element_type=jnp.float32)
        u = jnp.dot(x, wu_ref[...], preferred_element_type=jnp.float32)
        a = (g * jax.nn.sigmoid(g) * u).astype(MXU_DTYPE)
        o_ref[...] = jnp.dot(a, wd_ref[...], preferred_element_type=jnp.float32)

    @pl.when(pl.program_id(0) >= nu_ref[0])
    def _():
        o_ref[...] = jnp.zeros_like(o_ref)


def _experts(xs, lidx, block_e, n_used, w_gate, w_up, w_down):
    n_slots, d = xs.shape
    n_blocks = n_slots // MOE_BLOCK
    grid_spec = pltpu.PrefetchScalarGridSpec(
        num_scalar_prefetch=3,
        grid=(n_blocks,),
        in_specs=[
            pl.BlockSpec((MOE_BLOCK, d), lambda i, l, be, nu: (i, 0)),
            pl.BlockSpec((None, None, d, D_EXPERT), lambda i, l, be, nu: (l[0], be[i], 0, 0)),
            pl.BlockSpec((None, None, d, D_EXPERT), lambda i, l, be, nu: (l[0], be[i], 0, 0)),
            pl.BlockSpec((None, None, D_EXPERT, d), lambda i, l, be, nu: (l[0], be[i], 0, 0)),
        ],
        out_specs=pl.BlockSpec((MOE_BLOCK, d), lambda i, l, be, nu: (i, 0)),
    )
    return pl.pallas_call(
        _expert_kernel,
        grid_spec=grid_spec,
        out_shape=jax.ShapeDtypeStruct((n_slots, d), jnp.float32),
        compiler_params=_cparams("parallel"),
        name="experts",
    )(lidx, block_e, n_used, xs, w_gate, w_up, w_down)


def _rope_table(seq):
    half = ROT_DIM // 2
    inv = ROPE_THETA ** (-jnp.arange(half, dtype=jnp.float32) / half)
    ang = jnp.arange(seq).astype(jnp.float32)[:, None] * inv[None, :]
    cos, sin = jnp.cos(ang), jnp.sin(ang)
    ones = jnp.ones((seq, HEAD_DIM - ROT_DIM), jnp.float32)
    zeros = jnp.zeros((seq, HEAD_DIM - ROT_DIM), jnp.float32)
    zh = jnp.zeros((seq, half), jnp.float32)
    c = jnp.concatenate([cos, cos, ones], axis=1)
    s_up = jnp.concatenate([-sin, zh, zeros], axis=1)
    s_dn = jnp.concatenate([zh, sin, zeros], axis=1)
    rep = LANES // HEAD_DIM
    return jnp.concatenate([jnp.tile(c, (1, rep)), jnp.tile(s_up, (1, rep)), jnp.tile(s_dn, (1, rep))], axis=1)


def _split_hi_lo(w):
    hi = w.astype(MXU_DTYPE)
    lo = (w - hi.astype(jnp.float32)).astype(MXU_DTYPE)
    return jnp.concatenate([hi, lo], axis=-1)


def _trunk(x, p, consts):
    b, seq, d = x.shape
    t = b * seq
    n_assign = t * TOP_K
    n_blocks = -(-n_assign // MOE_BLOCK) + N_EXPERTS
    n_slots = n_blocks * MOE_BLOCK
    tok = jnp.broadcast_to(jnp.arange(t, dtype=jnp.int32)[None, :], (TOP_K, t))

    def layer(x2d, l):
        lidx = l.reshape(1)
        proj = _in_proj(x2d, lidx, p["ln_mix"], p["w_in"], p["head_gains"], consts["blockdiag"], consts["rope"], seq)
        proj = proj.reshape(b, seq, 3 * d)
        merged = None
        for n, dil in enumerate(DILATIONS):
            merged = _band_attn(proj, dil, consts["band"], merged, n == len(DILATIONS) - 1)
        y_d = merged[0].reshape(t, W_DIL)
        bias_tab = _na_bias_table(lax.dynamic_index_in_dim(p["rpb_na"], l, 0, keepdims=False))
        y_n = _na_attn(proj, bias_tab).reshape(t, W_NA)
        x2d = _out_proj(y_d, y_n, x2d, lidx, p["out_norm_dil"], p["out_norm_na"], p["w_out"], seq)

        h, eid, rank, gate, counts = _router(x2d, lidx, p["ln_ffn"], p["w_route"], p["b_route"], consts["tri"])
        counts = counts[:, 0].astype(jnp.int32)
        padded = (counts + MOE_BLOCK - 1) // MOE_BLOCK * MOE_BLOCK
        pend = jnp.cumsum(padded)
        pstart = pend - padded
        dest = pstart[eid] + rank
        block_e = jnp.minimum(jnp.sum(jnp.arange(n_blocks)[:, None] * MOE_BLOCK >= pend[None, :], axis=-1),
                              N_EXPERTS - 1).astype(jnp.int32)
        n_used = (pend[-1:] // MOE_BLOCK).astype(jnp.int32)
        slot_tok = jnp.zeros((n_slots,), jnp.int32).at[dest.reshape(-1)].set(tok.reshape(-1))
        xs = jnp.take(h, slot_tok, axis=0)
        yb = _experts(xs, lidx, block_e, n_used, p["w_gate"], p["w_up"], p["w_down"])
        y = jnp.take(yb, dest[0], axis=0) * gate[0][:, None] + jnp.take(yb, dest[1], axis=0) * gate[1][:, None]
        return x2d + y, None

    x2d, _ = lax.scan(layer, x.reshape(t, d), jnp.arange(DEPTH, dtype=jnp.int32))
    return x2d.reshape(b, seq, d)


def kernel(x_prompt, x_sample, ln_mix, w_in, q_norm_dil, k_norm_dil, q_norm_na, k_norm_na, rpb_na, out_norm_dil,
           out_norm_na, w_out, ln_ffn, w_router_group, b_router_group, w_router_expert, b_router_expert, w_gate,
           w_up, w_down):
    assert x_prompt.shape[1:] == x_sample.shape[1:]
    seq = x_prompt.shape[1]
    assert seq % (max(DILATIONS) * BAND_SIDE) == 0 and seq % (NA_ROWS_STEP * GRID_W) == 0
    depth = ln_mix.shape[0]
    heads_per_group = W_DIL // HEAD_DIM
    head_gains = jnp.stack([jnp.tile(g, (1, heads_per_group)) for g in (q_norm_dil, k_norm_dil, q_norm_na, k_norm_na)],
                           axis=1)
    w_exp = w_router_expert.transpose(0, 2, 1, 3).reshape(depth, D_MODEL, N_EXPERTS)
    w_route = jnp.concatenate([w_exp, w_router_group], axis=-1)
    w_route = jnp.pad(w_route, ((0, 0), (0, 0), (0, LANES - w_route.shape[-1])))
    b_route = jnp.concatenate([b_router_expert.reshape(depth, N_EXPERTS), b_router_group], axis=-1)
    b_route = jnp.pad(b_route, ((0, 0), (0, LANES - b_route.shape[-1])))[:, None, :]
    params = {
        "ln_mix": ln_mix[:, None, :],
        "w_in": w_in.astype(MXU_DTYPE),
        "head_gains": head_gains,
        "rpb_na": rpb_na,
        "out_norm_dil": out_norm_dil[:, None, :],
        "out_norm_na": out_norm_na[:, None, :],
        "w_out": w_out.astype(MXU_DTYPE),
        "ln_ffn": ln_ffn[:, None, :],
        "w_route": _split_hi_lo(w_route),
        "b_route": b_route,
        "w_gate": w_gate.astype(MXU_DTYPE),
        "w_up": w_up.astype(MXU_DTYPE),
        "w_down": w_down.astype(MXU_DTYPE),
    }
    head_of = np.arange(W_DIL) // HEAD_DIM
    tm_route = min(TM_ROUTE, (x_prompt.shape[0] + x_sample.shape[0]) * seq)
    consts = {
        "blockdiag": jnp.asarray((head_of[:, None] == head_of[None, :]) / HEAD_DIM, MXU_DTYPE),
        "rope": _rope_table(seq),
        "band": _band_bias(),
        "tri": jnp.asarray(np.arange(tm_route)[:, None] < np.arange(tm_route)[None, :], MXU_DTYPE),
    }
    n_prompt = x_prompt.shape[0]
    y = _trunk(jnp.concatenate([x_prompt, x_sample], axis=0), params, consts)
    return y[:n_prompt], y[n_prompt:]
```

```python
import functools

import numpy as np
import jax
import jax.numpy as jnp
from jax import lax
from jax.experimental import pallas as pl
from jax.experimental.pallas import tpu as pltpu

D_MODEL = 1024
DEPTH = 4
HEAD_DIM = 64
N_HEADS_DIL = 8
N_HEADS_NA = 8
W_DIL = N_HEADS_DIL * HEAD_DIM
W_NA = N_HEADS_NA * HEAD_DIM
QKV_DIL = 3 * W_DIL
DILATIONS = (1, 4, 16)
BAND_SIDE = 64
ROT_DIM = HEAD_DIM // 4
ROPE_THETA = 500000.0
GRID_W = 64
NB_ROWS = 8
NB_COLS = 16
N_GROUPS = 4
EXPERTS_PER_GROUP = 8
N_EXPERTS = N_GROUPS * EXPERTS_PER_GROUP
TOP_K = 2
D_EXPERT = D_MODEL // 2
EPS = 1e-6
NEG = -1e30

LANES = 128
MXU_DTYPE = jnp.bfloat16
VMEM_LIMIT = 48 * 1024 * 1024

TM_PROJ = 512
BAND_TILE = {1: 512, 4: 256, 16: 64}
SUB_BAND = 128
BAND_AHEAD = 3
NA_ROWS_STEP = 8
NA_ROWS_ITER = 2
NA_AHEAD = 3
TM_ROUTE = 512
MOE_BLOCK = 512


def _cparams(*sem):
    return pltpu.CompilerParams(dimension_semantics=sem, vmem_limit_bytes=VMEM_LIMIT)


def _in_proj_kernel(l_ref, x_ref, g_ref, w_ref, hg_ref, bd_ref, rope_ref, o_ref, *rest):
    del l_ref
    class_refs, stage_ref = rest[:-1], rest[-1]
    tm = x_ref.shape[0]
    x = x_ref[...]
    ms = jnp.mean(x * x, axis=-1, keepdims=True)
    h = (x * lax.rsqrt(ms + EPS) * g_ref[...]).astype(MXU_DTYPE)
    cos = rope_ref[:, 0:LANES]
    sin_up = rope_ref[:, LANES:2 * LANES]
    sin_dn = rope_ref[:, 2 * LANES:3 * LANES]
    half = ROT_DIM // 2
    for c in range(6):
        acc = jnp.dot(h, w_ref[:, c * W_DIL:(c + 1) * W_DIL], preferred_element_type=jnp.float32)
        if c in (0, 1, 3, 4):
            gi = (0, 1, None, 2, 3)[c]
            msh = jnp.dot((acc * acc).astype(MXU_DTYPE), bd_ref[...], preferred_element_type=jnp.float32)
            acc = acc * lax.rsqrt(msh + EPS) * hg_ref[gi:gi + 1, :]
        if c in (0, 1):
            parts = []
            for j in range(W_DIL // LANES):
                t = acc[:, j * LANES:(j + 1) * LANES]
                parts.append(t * cos + pltpu.roll(t, LANES - half, 1) * sin_up + pltpu.roll(t, half, 1) * sin_dn)
            acc = jnp.concatenate(parts, axis=1)
        if c in (0, 3):
            acc = acc * (HEAD_DIM ** -0.5)
        o_ref[:, c * W_DIL:(c + 1) * W_DIL] = acc.astype(o_ref.dtype)
        if c < 3:
            for j in range(W_DIL // LANES):
                stage_ref[j] = acc[:, j * LANES:(j + 1) * LANES]
            for dil, cls_ref in zip(DILATIONS[1:], class_refs):
                for r in range(dil):
                    for j in range(W_DIL // LANES):
                        col = r * QKV_DIL + c * W_DIL + j * LANES
                        rows = stage_ref[j, pl.ds(r, tm // dil, stride=dil), :]
                        cls_ref[:, col:col + LANES] = rows.astype(cls_ref.dtype)


def _in_proj(x2d, lidx, ln_g, w_in, head_gains, blockdiag, rope_tab, seq):
    t, d = x2d.shape
    tm = min(TM_PROJ, seq)
    n_seq_blocks = seq // tm
    class_dils = DILATIONS[1:]
    grid_spec = pltpu.PrefetchScalarGridSpec(
        num_scalar_prefetch=1,
        grid=(t // tm,),
        in_specs=[
            pl.BlockSpec((tm, d), lambda i, l: (i, 0)),
            pl.BlockSpec((None, 1, d), lambda i, l: (l[0], 0, 0)),
            pl.BlockSpec((None, d, 3 * d), lambda i, l: (l[0], 0, 0)),
            pl.BlockSpec((None, 4, W_DIL), lambda i, l: (l[0], 0, 0)),
            pl.BlockSpec((W_DIL, W_DIL), lambda i, l: (0, 0)),
            pl.BlockSpec((tm, 3 * LANES), lambda i, l: (i % n_seq_blocks, 0)),
        ],
        out_specs=[pl.BlockSpec((tm, 3 * d), lambda i, l: (i, 0))]
        + [pl.BlockSpec((tm // dil, dil * QKV_DIL), lambda i, l: (i, 0)) for dil in class_dils],
        scratch_shapes=[pltpu.VMEM((W_DIL // LANES, tm, LANES), jnp.float32)],
    )
    return pl.pallas_call(
        _in_proj_kernel,
        grid_spec=grid_spec,
        out_shape=[jax.ShapeDtypeStruct((t, 3 * d), MXU_DTYPE)]
        + [jax.ShapeDtypeStruct((t // dil, dil * QKV_DIL), MXU_DTYPE) for dil in class_dils],
        compiler_params=_cparams("parallel"),
        name="in_proj",
    )(lidx, x2d, ln_g, w_in, head_gains, blockdiag, rope_tab)


def _pair_rows(x, low_mask):
    zero = jnp.zeros_like(x)
    return jnp.concatenate([jnp.where(low_mask, x, zero), jnp.where(low_mask, zero, x)], axis=0)


def _softmax_pv(s, v):
    m = jnp.max(s, axis=-1, keepdims=True)
    p = jnp.exp(s - m)
    den = jnp.sum(p, axis=-1, keepdims=True)
    o = jnp.dot(p.astype(MXU_DTYPE), v, preferred_element_type=jnp.float32)
    return o * (1.0 / den), m + jnp.log(den)


def _band_kernel(*refs, dil, tq, sub, n_l, has_prev, is_last):
    cur_ref, before_ref, after_ref, band_ref = refs[:4]
    pos = 4
    if has_prev:
        po_ref, pl_ref = refs[pos:pos + 2]
        pos += 2
    o_ref = refs[pos]
    pos += 1
    if not is_last:
        lse_ref = refs[pos]

    li = pl.program_id(1)
    side = BAND_SIDE
    nk = sub + 2 * side
    low = lax.broadcasted_iota(jnp.int32, (1, LANES), 1) < HEAD_DIM
    col = lax.broadcasted_iota(jnp.int32, (1, nk), 1)
    n_sub = tq // sub

    def window(j, c0):
        cs = slice(c0, c0 + LANES)
        lo, hi = j * sub - side, (j + 1) * sub + side
        parts = [before_ref[:, cs]] if lo < 0 else []
        parts.append(cur_ref[max(lo, 0):min(hi, tq), cs])
        if hi > tq:
            parts.append(after_ref[:, cs])
        return jnp.concatenate(parts, axis=0)

    def scores(r, hp, j):
        q0 = r * QKV_DIL + hp * LANES
        q2 = _pair_rows(cur_ref[j * sub:(j + 1) * sub, q0:q0 + LANES], low)
        s = lax.dot_general(q2, window(j, q0 + W_DIL), (((1,), (1,)), ((), ())), preferred_element_type=jnp.float32)
        s = s + band_ref[...]
        if j == 0:
            s = jnp.where((col >= side) | (li > 0), s, NEG)
        if j == n_sub - 1:
            s = jnp.where((col < nk - side) | (li < n_l - 1), s, NEG)
        return s

    def finish(s, r, hp, j):
        o2, lse2 = _softmax_pv(s, window(j, r * QKV_DIL + 2 * W_DIL + hp * LANES))
        o = jnp.where(low, o2[:sub], o2[sub:])
        lse = jnp.where(low, lse2[:sub], lse2[sub:])
        if dil == 1:
            rows = slice(j * sub, (j + 1) * sub)
        else:
            rows = pl.ds(dil * j * sub + r, sub, stride=dil)
        if has_prev:
            o_a = po_ref[hp, rows, :]
            lse_a = pl_ref[hp, rows, :]
            mx = jnp.maximum(lse_a, lse)
            w_a = jnp.exp(lse_a - mx)
            w_b = jnp.exp(lse - mx)
            tot = w_a + w_b
            o = (w_a * o_a + w_b * o) * (1.0 / tot)
            lse = mx + jnp.log(tot)
        o_ref[hp, rows, :] = o
        if not is_last:
            lse_ref[hp, rows, :] = lse

    work = [(r, hp, j) for r in range(dil) for hp in range(W_DIL // LANES) for j in range(n_sub)]
    pending = [scores(*w) for w in work[:BAND_AHEAD]]
    for n, w in enumerate(work):
        if n + BAND_AHEAD < len(work):
            pending.append(scores(*work[n + BAND_AHEAD]))
        finish(pending.pop(0), *w)


def _band_attn(cls, dil, batch, seq, prev, is_last):
    l_len = seq // dil
    tq = min(BAND_TILE[dil], l_len)
    sub = min(SUB_BAND, tq)
    n_l = l_len // tq
    halo_per_tile = tq // BAND_SIDE
    n_halo = l_len // BAND_SIDE
    width = dil * QKV_DIL

    cur = pl.BlockSpec((tq, width), lambda bi, l: (bi * n_l + l, 0))
    before = pl.BlockSpec((BAND_SIDE, width),
                          lambda bi, l: (bi * n_halo + jnp.maximum(l * halo_per_tile - 1, 0), 0))
    after = pl.BlockSpec((BAND_SIDE, width),
                         lambda bi, l: (bi * n_halo + jnp.minimum((l + 1) * halo_per_tile, n_halo - 1), 0))
    n_pairs = W_DIL // LANES
    nat = pl.BlockSpec((n_pairs, dil * tq, LANES), lambda bi, l: (0, bi * n_l + l, 0))
    band = _band_bias(sub)
    in_specs = [cur, before, after, pl.BlockSpec(band.shape, lambda bi, l: (0, 0))]
    args = [cls, cls, cls, band]
    has_prev = prev is not None
    if has_prev:
        in_specs += [nat, nat]
        args += list(prev)
    o_shape = jax.ShapeDtypeStruct((n_pairs, batch * seq, LANES), jnp.float32)
    n_out = 1 if is_last else 2
    return pl.pallas_call(
        functools.partial(_band_kernel, dil=dil, tq=tq, sub=sub, n_l=n_l, has_prev=has_prev, is_last=is_last),
        grid=(batch, n_l),
        in_specs=in_specs,
        out_specs=[nat] * n_out,
        out_shape=[o_shape] * n_out,
        compiler_params=_cparams("parallel", "parallel"),
        name=f"band_attn_d{dil}",
    )(*args)


def _band_bias(sub):
    nk = sub + 2 * BAND_SIDE
    r = np.arange(2 * sub)[:, None] % sub
    c = np.arange(nk)[None, :]
    ok = (c - r >= 0) & (c - r <= 2 * BAND_SIDE)
    return jnp.asarray(np.where(ok, 0.0, NEG), jnp.float32)


def _na_kernel(q_ref, kc_ref, kp_ref, kn_ref, vc_ref, vp_ref, vn_ref, bias_ref, o_ref, kk_ref, vv_ref, *, n_rows):
    step = pl.program_id(1)
    halo = (NB_ROWS // 2) * GRID_W
    cur = NA_ROWS_STEP * GRID_W
    kk_ref[0:halo, :] = kp_ref[...]
    kk_ref[halo:halo + cur, :] = kc_ref[...]
    kk_ref[halo + cur:, :] = kn_ref[...]
    vv_ref[0:halo, :] = vp_ref[...]
    vv_ref[halo:halo + cur, :] = vc_ref[...]
    vv_ref[halo + cur:, :] = vn_ref[...]

    low = lax.broadcasted_iota(jnp.int32, (1, LANES), 1) < HEAD_DIM
    n_keys = NB_ROWS * GRID_W
    row_base = step * NA_ROWS_STEP

    def rows_iter(it, carry):
        work = []
        for u in range(NA_ROWS_ITER):
            i = it * NA_ROWS_ITER + u
            r = row_base + i
            r0 = jnp.clip(r - NB_ROWS // 2, 0, n_rows - NB_ROWS)
            start = pl.multiple_of((r0 - row_base + NB_ROWS // 2) * GRID_W, GRID_W)
            qrow = pl.multiple_of(i * GRID_W, GRID_W)
            work += [(hp, qrow, start, r - r0) for hp in range(W_NA // LANES)]

        def scores(hp, qrow, start, variant):
            cs = slice(hp * LANES, (hp + 1) * LANES)
            q2 = _pair_rows(q_ref[pl.ds(qrow, GRID_W), cs], low)
            keys = kk_ref[pl.ds(start, n_keys), cs]
            s = lax.dot_general(q2, keys, (((1,), (1,)), ((), ())), preferred_element_type=jnp.float32)
            return s + bias_ref[hp, variant]

        def finish(s, hp, qrow, start, variant):
            cs = slice(hp * LANES, (hp + 1) * LANES)
            o2, _ = _softmax_pv(s, vv_ref[pl.ds(start, n_keys), cs])
            o_ref[pl.ds(qrow, GRID_W), cs] = jnp.where(low, o2[:GRID_W], o2[GRID_W:])

        pending = [scores(*w) for w in work[:NA_AHEAD]]
        for n, w in enumerate(work):
            if n + NA_AHEAD < len(work):
                pending.append(scores(*work[n + NA_AHEAD]))
            finish(pending.pop(0), *w)
        return carry

    lax.fori_loop(0, NA_ROWS_STEP // NA_ROWS_ITER, rows_iter, 0)


def _na_attn(proj, bias_tab):
    b, seq, width = proj.shape
    n_rows = seq // GRID_W
    cur = NA_ROWS_STEP * GRID_W
    halo = (NB_ROWS // 2) * GRID_W
    halo_per_step = cur // halo
    n_halo = seq // halo
    n_pairs = W_NA // LANES

    def cur_spec(which):
        return pl.BlockSpec((None, cur, W_NA), lambda bi, i: (bi, i, which))

    def before(which):
        return pl.BlockSpec((None, halo, W_NA), lambda bi, i: (bi, jnp.maximum(i * halo_per_step - 1, 0), which))

    def after(which):
        return pl.BlockSpec((None, halo, W_NA),
                            lambda bi, i: (bi, jnp.minimum((i + 1) * halo_per_step, n_halo - 1), which))

    return pl.pallas_call(
        functools.partial(_na_kernel, n_rows=n_rows),
        grid=(b, n_rows // NA_ROWS_STEP),
        in_specs=[cur_spec(3), cur_spec(4), before(4), after(4), cur_spec(5), before(5), after(5),
                  pl.BlockSpec((n_pairs, NB_ROWS, 2 * GRID_W, NB_ROWS * GRID_W), lambda bi, i: (0, 0, 0, 0))],
        out_specs=pl.BlockSpec((None, cur, W_NA), lambda bi, i: (bi, i, 0)),
        out_shape=jax.ShapeDtypeStruct((b, seq, W_NA), jnp.float32),
        scratch_shapes=[pltpu.VMEM((cur + 2 * halo, W_NA), MXU_DTYPE),
                        pltpu.VMEM((cur + 2 * halo, W_NA), MXU_DTYPE)],
        compiler_params=_cparams("parallel", "parallel"),
        name="na_attn",
    )(proj, proj, proj, proj, proj, proj, proj, bias_tab)


def _na_bias_table(rpb):
    c = np.arange(GRID_W)
    wstart = np.clip(c - NB_COLS // 2, 0, GRID_W - NB_COLS)
    kc = np.arange(GRID_W)
    valid = (kc[None, :] >= wstart[:, None]) & (kc[None, :] < wstart[:, None] + NB_COLS)
    dc = np.clip(kc[None, :] - c[:, None] + NB_COLS - 1, 0, 2 * NB_COLS - 2)
    dr = np.arange(NB_ROWS)[None, :] - np.arange(NB_ROWS)[:, None] + NB_ROWS - 1
    tab = rpb.astype(jnp.float32)[:, dr][:, :, :, dc]
    tab = jnp.where(jnp.asarray(valid)[None, None, None], tab, NEG)
    tab = tab.transpose(0, 1, 3, 2, 4)
    n_pairs = N_HEADS_NA // 2
    tab = tab.reshape(n_pairs, 2, NB_ROWS, GRID_W, NB_ROWS * GRID_W).transpose(0, 2, 1, 3, 4)
    return tab.reshape(n_pairs, NB_ROWS, 2 * GRID_W, NB_ROWS * GRID_W)


def _out_proj_kernel(l_ref, yd_ref, yn_ref, x_ref, gd_ref, gn_ref, w_ref, o_ref):
    del l_ref

    def norm(y, g):
        ms = jnp.mean(y * y, axis=-1, keepdims=True)
        return (y * lax.rsqrt(ms + EPS) * g).astype(MXU_DTYPE)

    nd = norm(jnp.concatenate([yd_ref[j] for j in range(yd_ref.shape[0])], axis=1), gd_ref[...])
    nn = norm(yn_ref[...], gn_ref[...])
    y = jnp.dot(nd, w_ref[0:W_DIL, :], preferred_element_type=jnp.float32)
    y = y + jnp.dot(nn, w_ref[W_DIL:, :], preferred_element_type=jnp.float32)
    o_ref[...] = x_ref[...] + y


def _out_proj(y_d, y_n, x2d, lidx, g_d, g_n, w_out, seq):
    t, d = x2d.shape
    tm = min(TM_PROJ, seq)
    grid_spec = pltpu.PrefetchScalarGridSpec(
        num_scalar_prefetch=1,
        grid=(t // tm,),
        in_specs=[
            pl.BlockSpec((W_DIL // LANES, tm, LANES), lambda i, l: (0, i, 0)),
            pl.BlockSpec((tm, W_NA), lambda i, l: (i, 0)),
            pl.BlockSpec((tm, d), lambda i, l: (i, 0)),
            pl.BlockSpec((None, 1, W_DIL), lambda i, l: (l[0], 0, 0)),
            pl.BlockSpec((None, 1, W_NA), lambda i, l: (l[0], 0, 0)),
            pl.BlockSpec((None, d, d), lambda i, l: (l[0], 0, 0)),
        ],
        out_specs=pl.BlockSpec((tm, d), lambda i, l: (i, 0)),
    )
    return pl.pallas_call(
        _out_proj_kernel,
        grid_spec=grid_spec,
        out_shape=jax.ShapeDtypeStruct((t, d), jnp.float32),
        compiler_params=_cparams("parallel"),
        name="out_proj",
    )(lidx, y_d, y_n, x2d, g_d, g_n, w_out)


def _router_kernel(l_ref, x_ref, g_ref, w_ref, b_ref, tri_ref, h_ref, eid_ref, rank_ref, gate_ref, cnt_ref, carry_ref):
    del l_ref

    @pl.when(pl.program_id(0) == 0)
    def _():
        carry_ref[...] = jnp.zeros_like(carry_ref)

    x = x_ref[...]
    ms = jnp.mean(x * x, axis=-1, keepdims=True)
    h = x * lax.rsqrt(ms + EPS) * g_ref[...]
    h_hi = h.astype(MXU_DTYPE)
    h_ref[...] = h
    h_lo = (h - h_hi.astype(jnp.float32)).astype(MXU_DTYPE)
    both = jnp.dot(h_hi, w_ref[...], preferred_element_type=jnp.float32)
    logits = both[:, :LANES] + both[:, LANES:]
    logits = logits + jnp.dot(h_lo, w_ref[:, :LANES], preferred_element_type=jnp.float32) + b_ref[...]
    lt = logits.T
    tm = lt.shape[1]

    best = lt[N_EXPERTS:N_EXPERTS + 1]
    gsel = jnp.zeros_like(best)
    for g in range(1, N_GROUPS):
        cand = lt[N_EXPERTS + g:N_EXPERTS + g + 1]
        upd = cand > best
        gsel = jnp.where(upd, float(g), gsel)
        best = jnp.where(upd, cand, best)
    den = jnp.zeros_like(best)
    for g in range(N_GROUPS):
        den = den + jnp.exp(lt[N_EXPERTS + g:N_EXPERTS + g + 1] - best)
    g_gate = 1.0 / den

    e8 = EXPERTS_PER_GROUP
    sel = lt[0:e8]
    for g in range(1, N_GROUPS):
        sel = jnp.where(gsel == float(g), lt[g * e8:(g + 1) * e8], sel)
    row = lax.broadcasted_iota(jnp.int32, (e8, tm), 0).astype(jnp.float32)
    v1 = jnp.max(sel, axis=0, keepdims=True)
    i1 = jnp.min(jnp.where(sel == v1, row, float(e8)), axis=0, keepdims=True)
    rest = jnp.where(row == i1, -jnp.inf, sel)
    v2 = jnp.max(rest, axis=0, keepdims=True)
    i2 = jnp.min(jnp.where(rest == v2, row, float(e8)), axis=0, keepdims=True)
    e2 = jnp.exp(v2 - v1)
    inv = 1.0 / (1.0 + e2)
    gate_ref[0:1, :] = g_gate * inv
    gate_ref[1:2, :] = g_gate * (e2 * inv)
    eid1 = gsel * float(e8) + i1
    eid2 = gsel * float(e8) + i2
    eid_ref[0:1, :] = eid1.astype(jnp.int32)
    eid_ref[1:2, :] = eid2.astype(jnp.int32)

    erow = lax.broadcasted_iota(jnp.int32, (N_EXPERTS, tm), 0).astype(jnp.float32)
    hit1 = erow == eid1
    hit2 = erow == eid2
    oh = jnp.concatenate([hit1, hit2], axis=0).astype(jnp.float32)
    pref = jnp.dot(oh.astype(MXU_DTYPE), tri_ref[...], preferred_element_type=jnp.float32)
    tot = jnp.sum(oh, axis=1, keepdims=True)
    carry = carry_ref[:, 0:1]
    val1 = carry + pref[:N_EXPERTS]
    val2 = carry + tot[:N_EXPERTS] + pref[N_EXPERTS:]
    rank_ref[0:1, :] = jnp.sum(jnp.where(hit1, val1, 0.0), axis=0, keepdims=True).astype(jnp.int32)
    rank_ref[1:2, :] = jnp.sum(jnp.where(hit2, val2, 0.0), axis=0, keepdims=True).astype(jnp.int32)
    new_carry = carry_ref[...] + (tot[:N_EXPERTS] + tot[N_EXPERTS:])
    carry_ref[...] = new_carry
    cnt_ref[...] = new_carry


def _router(x2d, lidx, ln_g, w_route, b_route, tri):
    t, d = x2d.shape
    tm = min(TM_ROUTE, t)
    grid_spec = pltpu.PrefetchScalarGridSpec(
        num_scalar_prefetch=1,
        grid=(t // tm,),
        in_specs=[
            pl.BlockSpec((tm, d), lambda i, l: (i, 0)),
            pl.BlockSpec((None, 1, d), lambda i, l: (l[0], 0, 0)),
            pl.BlockSpec((None, d, 2 * LANES), lambda i, l: (l[0], 0, 0)),
            pl.BlockSpec((None, 1, LANES), lambda i, l: (l[0], 0, 0)),
            pl.BlockSpec((tm, tm), lambda i, l: (0, 0)),
        ],
        out_specs=[
            pl.BlockSpec((tm, d), lambda i, l: (i, 0)),
            pl.BlockSpec((TOP_K, tm), lambda i, l: (0, i)),
            pl.BlockSpec((TOP_K, tm), lambda i, l: (0, i)),
            pl.BlockSpec((TOP_K, tm), lambda i, l: (0, i)),
            pl.BlockSpec((N_EXPERTS, LANES), lambda i, l: (0, 0)),
        ],
        scratch_shapes=[pltpu.VMEM((N_EXPERTS, LANES), jnp.float32)],
    )
    return pl.pallas_call(
        _router_kernel,
        grid_spec=grid_spec,
        out_shape=[
            jax.ShapeDtypeStruct((t, d), jnp.float32),
            jax.ShapeDtypeStruct((TOP_K, t), jnp.int32),
            jax.ShapeDtypeStruct((TOP_K, t), jnp.int32),
            jax.ShapeDtypeStruct((TOP_K, t), jnp.float32),
            jax.ShapeDtypeStruct((N_EXPERTS, LANES), jnp.float32),
        ],
        compiler_params=_cparams("arbitrary"),
        name="router",
    )(lidx, x2d, ln_g, w_route, b_route, tri)


def _expert_kernel(l_ref, be_ref, nu_ref, x_ref, wg_ref, wu_ref, wd_ref, o_ref):
    del l_ref, be_ref

    @pl.when(pl.program_id(0) < nu_ref[0])
    def _():
        x = x_ref[...].astype(MXU_DTYPE)
        g = jnp.dot(x, wg_ref[...], preferred_element_type=jnp.float32)
        u = jnp.dot(x, wu_ref[...], preferred_element_type=jnp.float32)
        a = (g * jax.nn.sigmoid(g) * u).astype(MXU_DTYPE)
        o_ref[...] = jnp.dot(a, wd_ref[...], preferred_element_type=jnp.float32)

    @pl.when(pl.program_id(0) >= nu_ref[0])
    def _():
        o_ref[...] = jnp.zeros_like(o_ref)


def _experts(xs, lidx, block_e, n_used, w_gate, w_up, w_down):
    n_slots, d = xs.shape
    n_blocks = n_slots // MOE_BLOCK
    grid_spec = pltpu.PrefetchScalarGridSpec(
        num_scalar_prefetch=3,
        grid=(n_blocks,),
        in_specs=[
            pl.BlockSpec((MOE_BLOCK, d), lambda i, l, be, nu: (i, 0)),
            pl.BlockSpec((None, None, d, D_EXPERT), lambda i, l, be, nu: (l[0], be[i], 0, 0)),
            pl.BlockSpec((None, None, d, D_EXPERT), lambda i, l, be, nu: (l[0], be[i], 0, 0)),
            pl.BlockSpec((None, None, D_EXPERT, d), lambda i, l, be, nu: (l[0], be[i], 0, 0)),
        ],
        out_specs=pl.BlockSpec((MOE_BLOCK, d), lambda i, l, be, nu: (i, 0)),
    )
    return pl.pallas_call(
        _expert_kernel,
        grid_spec=grid_spec,
        out_shape=jax.ShapeDtypeStruct((n_slots, d), jnp.float32),
        compiler_params=_cparams("parallel"),
        name="experts",
    )(lidx, block_e, n_used, xs, w_gate, w_up, w_down)


def _rope_table(seq):
    half = ROT_DIM // 2
    inv = ROPE_THETA ** (-jnp.arange(half, dtype=jnp.float32) / half)
    ang = jnp.arange(seq).astype(jnp.float32)[:, None] * inv[None, :]
    cos, sin = jnp.cos(ang), jnp.sin(ang)
    ones = jnp.ones((seq, HEAD_DIM - ROT_DIM), jnp.float32)
    zeros = jnp.zeros((seq, HEAD_DIM - ROT_DIM), jnp.float32)
    zh = jnp.zeros((seq, half), jnp.float32)
    c = jnp.concatenate([cos, cos, ones], axis=1)
    s_up = jnp.concatenate([-sin, zh, zeros], axis=1)
    s_dn = jnp.concatenate([zh, sin, zeros], axis=1)
    rep = LANES // HEAD_DIM
    return jnp.concatenate([jnp.tile(c, (1, rep)), jnp.tile(s_up, (1, rep)), jnp.tile(s_dn, (1, rep))], axis=1)


def _split_hi_lo(w):
    hi = w.astype(MXU_DTYPE)
    lo = (w - hi.astype(jnp.float32)).astype(MXU_DTYPE)
    return jnp.concatenate([hi, lo], axis=-1)


def _trunk(x, p, consts):
    b, seq, d = x.shape
    t = b * seq
    n_assign = t * TOP_K
    n_blocks = -(-n_assign // MOE_BLOCK) + N_EXPERTS
    n_slots = n_blocks * MOE_BLOCK
    tok = jnp.broadcast_to(jnp.arange(t, dtype=jnp.int32)[None, :], (TOP_K, t))

    def layer(x2d, l):
        lidx = l.reshape(1)
        views = _in_proj(x2d, lidx, p["ln_mix"], p["w_in"], p["head_gains"], consts["blockdiag"], consts["rope"], seq)
        merged = None
        for n, dil in enumerate(DILATIONS):
            merged = _band_attn(views[n], dil, b, seq, merged, n == len(DILATIONS) - 1)
        y_d = merged[0]
        bias_tab = _na_bias_table(lax.dynamic_index_in_dim(p["rpb_na"], l, 0, keepdims=False))
        y_n = _na_attn(views[0].reshape(b, seq, 3 * d), bias_tab).reshape(t, W_NA)
        x2d = _out_proj(y_d, y_n, x2d, lidx, p["out_norm_dil"], p["out_norm_na"], p["w_out"], seq)

        h, eid, rank, gate, counts = _router(x2d, lidx, p["ln_ffn"], p["w_route"], p["b_route"], consts["tri"])
        counts = counts[:, 0].astype(jnp.int32)
        padded = (counts + MOE_BLOCK - 1) // MOE_BLOCK * MOE_BLOCK
        pend = jnp.cumsum(padded)
        pstart = pend - padded
        experts = jnp.arange(N_EXPERTS, dtype=jnp.int32)[:, None, None]
        dest = jnp.sum(jnp.where(eid[None] == experts, pstart[:, None, None], 0), axis=0) + rank
        block_e = jnp.minimum(jnp.sum(jnp.arange(n_blocks)[:, None] * MOE_BLOCK >= pend[None, :], axis=-1),
                              N_EXPERTS - 1).astype(jnp.int32)
        n_used = (pend[-1:] // MOE_BLOCK).astype(jnp.int32)
        slot_tok = jnp.zeros((n_slots,), jnp.int32).at[dest.reshape(-1)].set(
            tok.reshape(-1), mode="promise_in_bounds", unique_indices=True)
        xs = h.at[slot_tok].get(mode="promise_in_bounds")
        yb = _experts(xs, lidx, block_e, n_used, p["w_gate"], p["w_up"], p["w_down"])
        y = (yb.at[dest[0]].get(mode="promise_in_bounds") * gate[0][:, None]
             + yb.at[dest[1]].get(mode="promise_in_bounds") * gate[1][:, None])
        return x2d + y, None

    x2d, _ = lax.scan(layer, x.reshape(t, d), jnp.arange(DEPTH, dtype=jnp.int32))
    return x2d.reshape(b, seq, d)


def kernel(x_prompt, x_sample, ln_mix, w_in, q_norm_dil, k_norm_dil, q_norm_na, k_norm_na, rpb_na, out_norm_dil,
           out_norm_na, w_out, ln_ffn, w_router_group, b_router_group, w_router_expert, b_router_expert, w_gate,
           w_up, w_down):
    assert x_prompt.shape[1:] == x_sample.shape[1:]
    seq = x_prompt.shape[1]
    assert seq % (max(DILATIONS) * BAND_SIDE) == 0 and seq % (NA_ROWS_STEP * GRID_W) == 0
    depth = ln_mix.shape[0]
    heads_per_group = W_DIL // HEAD_DIM
    head_gains = jnp.stack([jnp.tile(g, (1, heads_per_group)) for g in (q_norm_dil, k_norm_dil, q_norm_na, k_norm_na)],
                           axis=1)
    w_exp = w_router_expert.transpose(0, 2, 1, 3).reshape(depth, D_MODEL, N_EXPERTS)
    w_route = jnp.concatenate([w_exp, w_router_group], axis=-1)
    w_route = jnp.pad(w_route, ((0, 0), (0, 0), (0, LANES - w_route.shape[-1])))
    b_route = jnp.concatenate([b_router_expert.reshape(depth, N_EXPERTS), b_router_group], axis=-1)
    b_route = jnp.pad(b_route, ((0, 0), (0, LANES - b_route.shape[-1])))[:, None, :]
    params = {
        "ln_mix": ln_mix[:, None, :],
        "w_in": w_in.astype(MXU_DTYPE),
        "head_gains": head_gains,
        "rpb_na": rpb_na,
        "out_norm_dil": out_norm_dil[:, None, :],
        "out_norm_na": out_norm_na[:, None, :],
        "w_out": w_out.astype(MXU_DTYPE),
        "ln_ffn": ln_ffn[:, None, :],
        "w_route": _split_hi_lo(w_route),
        "b_route": b_route,
        "w_gate": w_gate.astype(MXU_DTYPE),
        "w_up": w_up.astype(MXU_DTYPE),
        "w_down": w_down.astype(MXU_DTYPE),
    }
    head_of = np.arange(W_DIL) // HEAD_DIM
    tm_route = min(TM_ROUTE, (x_prompt.shape[0] + x_sample.shape[0]) * seq)
    consts = {
        "blockdiag": jnp.asarray((head_of[:, None] == head_of[None, :]) / HEAD_DIM, MXU_DTYPE),
        "rope": _rope_table(seq),
        "tri": jnp.asarray(np.arange(tm_route)[:, None] < np.arange(tm_route)[None, :], MXU_DTYPE),
    }
    n_prompt = x_prompt.shape[0]
    y = _trunk(jnp.concatenate([x_prompt, x_sample], axis=0), params, consts)
    return y[:n_prompt], y[n_prompt:]
```

```python
import functools

import numpy as np
import jax
import jax.numpy as jnp
from jax import lax
from jax.experimental import pallas as pl
from jax.experimental.pallas import tpu as pltpu

D_MODEL = 1024
DEPTH = 4
HEAD_DIM = 64
N_HEADS_DIL = 8
N_HEADS_NA = 8
W_DIL = N_HEADS_DIL * HEAD_DIM
W_NA = N_HEADS_NA * HEAD_DIM
QKV_DIL = 3 * W_DIL
DILATIONS = (1, 4, 16)
BAND_SIDE = 64
ROT_DIM = HEAD_DIM // 4
ROPE_THETA = 500000.0
GRID_W = 64
NB_ROWS = 8
NB_COLS = 16
N_GROUPS = 4
EXPERTS_PER_GROUP = 8
N_EXPERTS = N_GROUPS * EXPERTS_PER_GROUP
TOP_K = 2
D_EXPERT = D_MODEL // 2
EPS = 1e-6
NEG = -1e30

LANES = 128
MXU_DTYPE = jnp.bfloat16
VMEM_LIMIT = 48 * 1024 * 1024

TM_PROJ = 512
BAND_TILE = {1: 512, 4: 256, 16: 64}
SUB_BAND = 128
BAND_SKEW = {1: (3, 1), 4: (3, 1), 16: (6, 2)}
NA_ROWS_STEP = 8
NA_ROWS_ITER = 2
NA_SKEW = (3, 1)
TM_ROUTE = 512
MOE_BLOCK = 512


def _cparams(*sem):
    return pltpu.CompilerParams(dimension_semantics=sem, vmem_limit_bytes=VMEM_LIMIT)


def _in_proj_kernel(l_ref, x_ref, g_ref, w_ref, hg_ref, bd_ref, rope_ref, o_ref, *rest):
    del l_ref
    class_refs, stage_ref = rest[:-1], rest[-1]
    tm = x_ref.shape[0]
    x = x_ref[...]
    ms = jnp.mean(x * x, axis=-1, keepdims=True)
    h = (x * lax.rsqrt(ms + EPS) * g_ref[...]).astype(MXU_DTYPE)
    cos = rope_ref[:, 0:LANES]
    sin_up = rope_ref[:, LANES:2 * LANES]
    sin_dn = rope_ref[:, 2 * LANES:3 * LANES]
    half = ROT_DIM // 2
    for c in range(6):
        acc = jnp.dot(h, w_ref[:, c * W_DIL:(c + 1) * W_DIL], preferred_element_type=jnp.float32)
        if c in (0, 1, 3, 4):
            gi = (0, 1, None, 2, 3)[c]
            msh = jnp.dot((acc * acc).astype(MXU_DTYPE), bd_ref[...], preferred_element_type=jnp.float32)
            acc = acc * lax.rsqrt(msh + EPS) * hg_ref[gi:gi + 1, :]
        if c in (0, 1):
            parts = []
            for j in range(W_DIL // LANES):
                t = acc[:, j * LANES:(j + 1) * LANES]
                parts.append(t * cos + pltpu.roll(t, LANES - half, 1) * sin_up + pltpu.roll(t, half, 1) * sin_dn)
            acc = jnp.concatenate(parts, axis=1)
        if c in (0, 3):
            acc = acc * (HEAD_DIM ** -0.5)
        o_ref[:, c * W_DIL:(c + 1) * W_DIL] = acc.astype(o_ref.dtype)
        if c < 3:
            for j in range(W_DIL // LANES):
                stage_ref[j] = acc[:, j * LANES:(j + 1) * LANES]
            for dil, cls_ref in zip(DILATIONS[1:], class_refs):
                for r in range(dil):
                    for j in range(W_DIL // LANES):
                        col = r * QKV_DIL + c * W_DIL + j * LANES
                        rows = stage_ref[j, pl.ds(r, tm // dil, stride=dil), :]
                        cls_ref[:, col:col + LANES] = rows.astype(cls_ref.dtype)


def _in_proj(x2d, lidx, ln_g, w_in, head_gains, blockdiag, rope_tab, seq):
    t, d = x2d.shape
    tm = min(TM_PROJ, seq)
    n_seq_blocks = seq // tm
    class_dils = DILATIONS[1:]
    grid_spec = pltpu.PrefetchScalarGridSpec(
        num_scalar_prefetch=1,
        grid=(t // tm,),
        in_specs=[
            pl.BlockSpec((tm, d), lambda i, l: (i, 0)),
            pl.BlockSpec((None, 1, d), lambda i, l: (l[0], 0, 0)),
            pl.BlockSpec((None, d, 3 * d), lambda i, l: (l[0], 0, 0)),
            pl.BlockSpec((None, 4, W_DIL), lambda i, l: (l[0], 0, 0)),
            pl.BlockSpec((W_DIL, W_DIL), lambda i, l: (0, 0)),
            pl.BlockSpec((tm, 3 * LANES), lambda i, l: (i % n_seq_blocks, 0)),
        ],
        out_specs=[pl.BlockSpec((tm, 3 * d), lambda i, l: (i, 0))]
        + [pl.BlockSpec((tm // dil, dil * QKV_DIL), lambda i, l: (i, 0)) for dil in class_dils],
        scratch_shapes=[pltpu.VMEM((W_DIL // LANES, tm, LANES), jnp.float32)],
    )
    return pl.pallas_call(
        _in_proj_kernel,
        grid_spec=grid_spec,
        out_shape=[jax.ShapeDtypeStruct((t, 3 * d), MXU_DTYPE)]
        + [jax.ShapeDtypeStruct((t // dil, dil * QKV_DIL), MXU_DTYPE) for dil in class_dils],
        compiler_params=_cparams("parallel"),
        name="in_proj",
    )(lidx, x2d, ln_g, w_in, head_gains, blockdiag, rope_tab)


def _pair_rows(x, low_mask):
    zero = jnp.zeros_like(x)
    return jnp.concatenate([jnp.where(low_mask, x, zero), jnp.where(low_mask, zero, x)], axis=0)


def _softmax_pv(s, v):
    m = jnp.max(s, axis=-1, keepdims=True)
    p = jnp.exp(s - m)
    den = jnp.sum(p, axis=-1, keepdims=True)
    return jnp.dot(p.astype(MXU_DTYPE), v, preferred_element_type=jnp.float32), den, m


def _staged(work, scores, attend, emit, ahead, lag):
    s_queue, a_queue = [], []
    for step in range(len(work) + ahead + lag):
        if step < len(work):
            s_queue.append(scores(*work[step]))
        if 0 <= step - ahead < len(work):
            a_queue.append(attend(s_queue.pop(0), *work[step - ahead]))
        if 0 <= step - ahead - lag < len(work):
            emit(a_queue.pop(0), *work[step - ahead - lag])


def _band_kernel(*refs, dil, tq, sub, n_l, has_prev, is_last):
    cur_ref, before_ref, after_ref, band_ref = refs[:4]
    pos = 4
    if has_prev:
        po_ref, pl_ref = refs[pos:pos + 2]
        pos += 2
    o_ref = refs[pos]
    pos += 1
    if not is_last:
        lse_ref = refs[pos]

    li = pl.program_id(1)
    side = BAND_SIDE
    nk = sub + 2 * side
    low = lax.broadcasted_iota(jnp.int32, (1, LANES), 1) < HEAD_DIM
    col = lax.broadcasted_iota(jnp.int32, (1, nk), 1)
    n_sub = tq // sub

    def window(j, c0):
        cs = slice(c0, c0 + LANES)
        lo, hi = j * sub - side, (j + 1) * sub + side
        parts = [before_ref[:, cs]] if lo < 0 else []
        parts.append(cur_ref[max(lo, 0):min(hi, tq), cs])
        if hi > tq:
            parts.append(after_ref[:, cs])
        return jnp.concatenate(parts, axis=0)

    def scores(r, hp, j):
        q0 = r * QKV_DIL + hp * LANES
        q2 = _pair_rows(cur_ref[j * sub:(j + 1) * sub, q0:q0 + LANES], low)
        s = lax.dot_general(q2, window(j, q0 + W_DIL), (((1,), (1,)), ((), ())), preferred_element_type=jnp.float32)
        s = s + band_ref[...]
        if j == 0:
            s = jnp.where((col >= side) | (li > 0), s, NEG)
        if j == n_sub - 1:
            s = jnp.where((col < nk - side) | (li < n_l - 1), s, NEG)
        return s

    def attend(s, r, hp, j):
        return _softmax_pv(s, window(j, r * QKV_DIL + 2 * W_DIL + hp * LANES))

    def emit(res, r, hp, j):
        o2, den, m = res
        o2 = o2 * (1.0 / den)
        lse2 = m + jnp.log(den)
        o = jnp.where(low, o2[:sub], o2[sub:])
        lse = jnp.where(low, lse2[:sub], lse2[sub:])
        if dil == 1:
            rows = slice(j * sub, (j + 1) * sub)
        else:
            rows = pl.ds(dil * j * sub + r, sub, stride=dil)
        if has_prev:
            o_a = po_ref[hp, rows, :]
            lse_a = pl_ref[hp, rows, :]
            mx = jnp.maximum(lse_a, lse)
            w_a = jnp.exp(lse_a - mx)
            w_b = jnp.exp(lse - mx)
            tot = w_a + w_b
            o = (w_a * o_a + w_b * o) * (1.0 / tot)
            lse = mx + jnp.log(tot)
        o_ref[hp, rows, :] = o
        if not is_last:
            lse_ref[hp, rows, :] = lse

    work = [(r, hp, j) for r in range(dil) for hp in range(W_DIL // LANES) for j in range(n_sub)]
    ahead, lag = BAND_SKEW[dil]
    _staged(work, scores, attend, emit, ahead, lag)


def _band_attn(cls, dil, batch, seq, prev, is_last):
    l_len = seq // dil
    tq = min(BAND_TILE[dil], l_len)
    sub = min(SUB_BAND, tq)
    n_l = l_len // tq
    halo_per_tile = tq // BAND_SIDE
    n_halo = l_len // BAND_SIDE
    width = dil * QKV_DIL

    cur = pl.BlockSpec((tq, width), lambda bi, l: (bi * n_l + l, 0))
    before = pl.BlockSpec((BAND_SIDE, width),
                          lambda bi, l: (bi * n_halo + jnp.maximum(l * halo_per_tile - 1, 0), 0))
    after = pl.BlockSpec((BAND_SIDE, width),
                         lambda bi, l: (bi * n_halo + jnp.minimum((l + 1) * halo_per_tile, n_halo - 1), 0))
    n_pairs = W_DIL // LANES
    nat = pl.BlockSpec((n_pairs, dil * tq, LANES), lambda bi, l: (0, bi * n_l + l, 0))
    band = _band_bias(sub)
    in_specs = [cur, before, after, pl.BlockSpec(band.shape, lambda bi, l: (0, 0))]
    args = [cls, cls, cls, band]
    has_prev = prev is not None
    if has_prev:
        in_specs += [nat, nat]
        args += list(prev)
    o_shape = jax.ShapeDtypeStruct((n_pairs, batch * seq, LANES), jnp.float32)
    n_out = 1 if is_last else 2
    return pl.pallas_call(
        functools.partial(_band_kernel, dil=dil, tq=tq, sub=sub, n_l=n_l, has_prev=has_prev, is_last=is_last),
        grid=(batch, n_l),
        in_specs=in_specs,
        out_specs=[nat] * n_out,
        out_shape=[o_shape] * n_out,
        compiler_params=_cparams("parallel", "parallel"),
        name=f"band_attn_d{dil}",
    )(*args)


def _band_bias(sub):
    nk = sub + 2 * BAND_SIDE
    r = np.arange(2 * sub)[:, None] % sub
    c = np.arange(nk)[None, :]
    ok = (c - r >= 0) & (c - r <= 2 * BAND_SIDE)
    return jnp.asarray(np.where(ok, 0.0, NEG), jnp.float32)


def _na_kernel(q_ref, kc_ref, kp_ref, kn_ref, vc_ref, vp_ref, vn_ref, bias_ref, o_ref, kk_ref, vv_ref, *, n_rows):
    step = pl.program_id(1)
    halo = (NB_ROWS // 2) * GRID_W
    cur = NA_ROWS_STEP * GRID_W
    kk_ref[0:halo, :] = kp_ref[...]
    kk_ref[halo:halo + cur, :] = kc_ref[...]
    kk_ref[halo + cur:, :] = kn_ref[...]
    vv_ref[0:halo, :] = vp_ref[...]
    vv_ref[halo:halo + cur, :] = vc_ref[...]
    vv_ref[halo + cur:, :] = vn_ref[...]

    low = lax.broadcasted_iota(jnp.int32, (1, LANES), 1) < HEAD_DIM
    n_keys = NB_ROWS * GRID_W
    row_base = step * NA_ROWS_STEP

    def rows_iter(it, carry):
        work = []
        for u in range(NA_ROWS_ITER):
            i = it * NA_ROWS_ITER + u
            r = row_base + i
            r0 = jnp.clip(r - NB_ROWS // 2, 0, n_rows - NB_ROWS)
            start = pl.multiple_of((r0 - row_base + NB_ROWS // 2) * GRID_W, GRID_W)
            qrow = pl.multiple_of(i * GRID_W, GRID_W)
            work += [(hp, qrow, start, r - r0) for hp in range(W_NA // LANES)]

        def scores(hp, qrow, start, variant):
            cs = slice(hp * LANES, (hp + 1) * LANES)
            q2 = _pair_rows(q_ref[pl.ds(qrow, GRID_W), cs], low)
            keys = kk_ref[pl.ds(start, n_keys), cs]
            s = lax.dot_general(q2, keys, (((1,), (1,)), ((), ())), preferred_element_type=jnp.float32)
            return s + bias_ref[hp, variant]

        def attend(s, hp, qrow, start, variant):
            return _softmax_pv(s, vv_ref[pl.ds(start, n_keys), hp * LANES:(hp + 1) * LANES])

        def emit(res, hp, qrow, start, variant):
            o2, den, _ = res
            o2 = o2 * (1.0 / den)
            o_ref[pl.ds(qrow, GRID_W), hp * LANES:(hp + 1) * LANES] = jnp.where(low, o2[:GRID_W], o2[GRID_W:])

        _staged(work, scores, attend, emit, *NA_SKEW)
        return carry

    lax.fori_loop(0, NA_ROWS_STEP // NA_ROWS_ITER, rows_iter, 0)


def _na_attn(proj, bias_tab):
    b, seq, width = proj.shape
    n_rows = seq // GRID_W
    cur = NA_ROWS_STEP * GRID_W
    halo = (NB_ROWS // 2) * GRID_W
    halo_per_step = cur // halo
    n_halo = seq // halo
    n_pairs = W_NA // LANES

    def cur_spec(which):
        return pl.BlockSpec((None, cur, W_NA), lambda bi, i: (bi, i, which))

    def before(which):
        return pl.BlockSpec((None, halo, W_NA), lambda bi, i: (bi, jnp.maximum(i * halo_per_step - 1, 0), which))

    def after(which):
        return pl.BlockSpec((None, halo, W_NA),
                            lambda bi, i: (bi, jnp.minimum((i + 1) * halo_per_step, n_halo - 1), which))

    return pl.pallas_call(
        functools.partial(_na_kernel, n_rows=n_rows),
        grid=(b, n_rows // NA_ROWS_STEP),
        in_specs=[cur_spec(3), cur_spec(4), before(4), after(4), cur_spec(5), before(5), after(5),
                  pl.BlockSpec((n_pairs, NB_ROWS, 2 * GRID_W, NB_ROWS * GRID_W), lambda bi, i: (0, 0, 0, 0))],
        out_specs=pl.BlockSpec((None, cur, W_NA), lambda bi, i: (bi, i, 0)),
        out_shape=jax.ShapeDtypeStruct((b, seq, W_NA), jnp.float32),
        scratch_shapes=[pltpu.VMEM((cur + 2 * halo, W_NA), MXU_DTYPE),
                        pltpu.VMEM((cur + 2 * halo, W_NA), MXU_DTYPE)],
        compiler_params=_cparams("parallel", "parallel"),
        name="na_attn",
    )(proj, proj, proj, proj, proj, proj, proj, bias_tab)


def _na_bias_table(rpb):
    c = np.arange(GRID_W)
    wstart = np.clip(c - NB_COLS // 2, 0, GRID_W - NB_COLS)
    kc = np.arange(GRID_W)
    valid = (kc[None, :] >= wstart[:, None]) & (kc[None, :] < wstart[:, None] + NB_COLS)
    dc = np.clip(kc[None, :] - c[:, None] + NB_COLS - 1, 0, 2 * NB_COLS - 2)
    dr = np.arange(NB_ROWS)[None, :] - np.arange(NB_ROWS)[:, None] + NB_ROWS - 1
    tab = rpb.astype(jnp.float32)[:, dr][:, :, :, dc]
    tab = jnp.where(jnp.asarray(valid)[None, None, None], tab, NEG)
    tab = tab.transpose(0, 1, 3, 2, 4)
    n_pairs = N_HEADS_NA // 2
    tab = tab.reshape(n_pairs, 2, NB_ROWS, GRID_W, NB_ROWS * GRID_W).transpose(0, 2, 1, 3, 4)
    return tab.reshape(n_pairs, NB_ROWS, 2 * GRID_W, NB_ROWS * GRID_W)


def _out_proj_kernel(l_ref, yd_ref, yn_ref, x_ref, gd_ref, gn_ref, w_ref, o_ref):
    del l_ref

    def norm(y, g):
        ms = jnp.mean(y * y, axis=-1, keepdims=True)
        return (y * lax.rsqrt(ms + EPS) * g).astype(MXU_DTYPE)

    nd = norm(jnp.concatenate([yd_ref[j] for j in range(yd_ref.shape[0])], axis=1), gd_ref[...])
    nn = norm(yn_ref[...], gn_ref[...])
    y = jnp.dot(nd, w_ref[0:W_DIL, :], preferred_element_type=jnp.float32)
    y = y + jnp.dot(nn, w_ref[W_DIL:, :], preferred_element_type=jnp.float32)
    o_ref[...] = x_ref[...] + y


def _out_proj(y_d, y_n, x2d, lidx, g_d, g_n, w_out, seq):
    t, d = x2d.shape
    tm = min(TM_PROJ, seq)
    grid_spec = pltpu.PrefetchScalarGridSpec(
        num_scalar_prefetch=1,
        grid=(t // tm,),
        in_specs=[
            pl.BlockSpec((W_DIL // LANES, tm, LANES), lambda i, l: (0, i, 0)),
            pl.BlockSpec((tm, W_NA), lambda i, l: (i, 0)),
            pl.BlockSpec((tm, d), lambda i, l: (i, 0)),
            pl.BlockSpec((None, 1, W_DIL), lambda i, l: (l[0], 0, 0)),
            pl.BlockSpec((None, 1, W_NA), lambda i, l: (l[0], 0, 0)),
            pl.BlockSpec((None, d, d), lambda i, l: (l[0], 0, 0)),
        ],
        out_specs=pl.BlockSpec((tm, d), lambda i, l: (i, 0)),
    )
    return pl.pallas_call(
        _out_proj_kernel,
        grid_spec=grid_spec,
        out_shape=jax.ShapeDtypeStruct((t, d), jnp.float32),
        compiler_params=_cparams("parallel"),
        name="out_proj",
    )(lidx, y_d, y_n, x2d, g_d, g_n, w_out)


def _router_kernel(l_ref, x_ref, g_ref, w_ref, b_ref, tri_ref, h_ref, eid_ref, rank_ref, gate_ref, cnt_ref, carry_ref):
    del l_ref

    @pl.when(pl.program_id(0) == 0)
    def _():
        carry_ref[...] = jnp.zeros_like(carry_ref)

    x = x_ref[...]
    ms = jnp.mean(x * x, axis=-1, keepdims=True)
    h = x * lax.rsqrt(ms + EPS) * g_ref[...]
    h_hi = h.astype(MXU_DTYPE)
    h_ref[...] = h
    h_lo = (h - h_hi.astype(jnp.float32)).astype(MXU_DTYPE)
    both = jnp.dot(h_hi, w_ref[...], preferred_element_type=jnp.float32)
    logits = both[:, :LANES] + both[:, LANES:]
    logits = logits + jnp.dot(h_lo, w_ref[:, :LANES], preferred_element_type=jnp.float32) + b_ref[...]
    lt = logits.T
    tm = lt.shape[1]

    best = lt[N_EXPERTS:N_EXPERTS + 1]
    gsel = jnp.zeros_like(best)
    for g in range(1, N_GROUPS):
        cand = lt[N_EXPERTS + g:N_EXPERTS + g + 1]
        upd = cand > best
        gsel = jnp.where(upd, float(g), gsel)
        best = jnp.where(upd, cand, best)
    den = jnp.zeros_like(best)
    for g in range(N_GROUPS):
        den = den + jnp.exp(lt[N_EXPERTS + g:N_EXPERTS + g + 1] - best)
    g_gate = 1.0 / den

    e8 = EXPERTS_PER_GROUP
    sel = lt[0:e8]
    for g in range(1, N_GROUPS):
        sel = jnp.where(gsel == float(g), lt[g * e8:(g + 1) * e8], sel)
    row = lax.broadcasted_iota(jnp.int32, (e8, tm), 0).astype(jnp.float32)
    v1 = jnp.max(sel, axis=0, keepdims=True)
    i1 = jnp.min(jnp.where(sel == v1, row, float(e8)), axis=0, keepdims=True)
    rest = jnp.where(row == i1, -jnp.inf, sel)
    v2 = jnp.max(rest, axis=0, keepdims=True)
    i2 = jnp.min(jnp.where(rest == v2, row, float(e8)), axis=0, keepdims=True)
    e2 = jnp.exp(v2 - v1)
    inv = 1.0 / (1.0 + e2)
    gate_ref[0:1, :] = g_gate * inv
    gate_ref[1:2, :] = g_gate * (e2 * inv)
    eid1 = gsel * float(e8) + i1
    eid2 = gsel * float(e8) + i2
    eid_ref[0:1, :] = eid1.astype(jnp.int32)
    eid_ref[1:2, :] = eid2.astype(jnp.int32)

    erow = lax.broadcasted_iota(jnp.int32, (N_EXPERTS, tm), 0).astype(jnp.float32)
    hit1 = erow == eid1
    hit2 = erow == eid2
    oh = jnp.concatenate([hit1, hit2], axis=0).astype(jnp.float32)
    pref = jnp.dot(oh.astype(MXU_DTYPE), tri_ref[...], preferred_element_type=jnp.float32)
    tot = jnp.sum(oh, axis=1, keepdims=True)
    carry = carry_ref[:, 0:1]
    val1 = carry + pref[:N_EXPERTS]
    val2 = carry + tot[:N_EXPERTS] + pref[N_EXPERTS:]
    rank_ref[0:1, :] = jnp.sum(jnp.where(hit1, val1, 0.0), axis=0, keepdims=True).astype(jnp.int32)
    rank_ref[1:2, :] = jnp.sum(jnp.where(hit2, val2, 0.0), axis=0, keepdims=True).astype(jnp.int32)
    new_carry = carry_ref[...] + (tot[:N_EXPERTS] + tot[N_EXPERTS:])
    carry_ref[...] = new_carry
    cnt_ref[...] = new_carry


def _router(x2d, lidx, ln_g, w_route, b_route, tri):
    t, d = x2d.shape
    tm = min(TM_ROUTE, t)
    grid_spec = pltpu.PrefetchScalarGridSpec(
        num_scalar_prefetch=1,
        grid=(t // tm,),
        in_specs=[
            pl.BlockSpec((tm, d), lambda i, l: (i, 0)),
            pl.BlockSpec((None, 1, d), lambda i, l: (l[0], 0, 0)),
            pl.BlockSpec((None, d, 2 * LANES), lambda i, l: (l[0], 0, 0)),
            pl.BlockSpec((None, 1, LANES), lambda i, l: (l[0], 0, 0)),
            pl.BlockSpec((tm, tm), lambda i, l: (0, 0)),
        ],
        out_specs=[
            pl.BlockSpec((tm, d), lambda i, l: (i, 0)),
            pl.BlockSpec((TOP_K, tm), lambda i, l: (0, i)),
            pl.BlockSpec((TOP_K, tm), lambda i, l: (0, i)),
            pl.BlockSpec((TOP_K, tm), lambda i, l: (0, i)),
            pl.BlockSpec((N_EXPERTS, LANES), lambda i, l: (0, 0)),
        ],
        scratch_shapes=[pltpu.VMEM((N_EXPERTS, LANES), jnp.float32)],
    )
    return pl.pallas_call(
        _router_kernel,
        grid_spec=grid_spec,
        out_shape=[
            jax.ShapeDtypeStruct((t, d), jnp.float32),
            jax.ShapeDtypeStruct((TOP_K, t), jnp.int32),
            jax.ShapeDtypeStruct((TOP_K, t), jnp.int32),
            jax.ShapeDtypeStruct((TOP_K, t), jnp.float32),
            jax.ShapeDtypeStruct((N_EXPERTS, LANES), jnp.float32),
        ],
        compiler_params=_cparams("arbitrary"),
        name="router",
    )(lidx, x2d, ln_g, w_route, b_route, tri)


def _expert_kernel(l_ref, be_ref, nu_ref, x_ref, wg_ref, wu_ref, wd_ref, o_ref):
    del l_ref, be_ref

    @pl.when(pl.program_id(0) < nu_ref[0])
    def _():
        x = x_ref[...].astype(MXU_DTYPE)
        g = jnp.dot(x, wg_ref[...], preferred_element_type=jnp.float32)
        u = jnp.dot(x, wu_ref[...], preferred_element_type=jnp.float32)
        a = (g * jax.nn.sigmoid(g) * u).astype(MXU_DTYPE)
        o_ref[...] = jnp.dot(a, wd_ref[...], preferred_element_type=jnp.float32)

    @pl.when(pl.program_id(0) >= nu_ref[0])
    def _():
        o_ref[...] = jnp.zeros_like(o_ref)


def _experts(xs, lidx, block_e, n_used, w_gate, w_up, w_down):
    n_slots, d = xs.shape
    n_blocks = n_slots // MOE_BLOCK
    grid_spec = pltpu.PrefetchScalarGridSpec(
        num_scalar_prefetch=3,
        grid=(n_blocks,),
        in_specs=[
            pl.BlockSpec((MOE_BLOCK, d), lambda i, l, be, nu: (i, 0)),
            pl.BlockSpec((None, None, d, D_EXPERT), lambda i, l, be, nu: (l[0], be[i], 0, 0)),
            pl.BlockSpec((None, None, d, D_EXPERT), lambda i, l, be, nu: (l[0], be[i], 0, 0)),
            pl.BlockSpec((None, None, D_EXPERT, d), lambda i, l, be, nu: (l[0], be[i], 0, 0)),
        ],
        out_specs=pl.BlockSpec((MOE_BLOCK, d), lambda i, l, be, nu: (i, 0)),
    )
    return pl.pallas_call(
        _expert_kernel,
        grid_spec=grid_spec,
        out_shape=jax.ShapeDtypeStruct((n_slots, d), jnp.float32),
        compiler_params=_cparams("parallel"),
        name="experts",
    )(lidx, block_e, n_used, xs, w_gate, w_up, w_down)


def _rope_table(seq):
    half = ROT_DIM // 2
    inv = ROPE_THETA ** (-jnp.arange(half, dtype=jnp.float32) / half)
    ang = jnp.arange(seq).astype(jnp.float32)[:, None] * inv[None, :]
    cos, sin = jnp.cos(ang), jnp.sin(ang)
    ones = jnp.ones((seq, HEAD_DIM - ROT_DIM), jnp.float32)
    zeros = jnp.zeros((seq, HEAD_DIM - ROT_DIM), jnp.float32)
    zh = jnp.zeros((seq, half), jnp.float32)
    c = jnp.concatenate([cos, cos, ones], axis=1)
    s_up = jnp.concatenate([-sin, zh, zeros], axis=1)
    s_dn = jnp.concatenate([zh, sin, zeros], axis=1)
    rep = LANES // HEAD_DIM
    return jnp.concatenate([jnp.tile(c, (1, rep)), jnp.tile(s_up, (1, rep)), jnp.tile(s_dn, (1, rep))], axis=1)


def _split_hi_lo(w):
    hi = w.astype(MXU_DTYPE)
    lo = (w - hi.astype(jnp.float32)).astype(MXU_DTYPE)
    return jnp.concatenate([hi, lo], axis=-1)


def _trunk(x, p, consts):
    b, seq, d = x.shape
    t = b * seq
    n_assign = t * TOP_K
    n_blocks = -(-n_assign // MOE_BLOCK) + N_EXPERTS
    n_slots = n_blocks * MOE_BLOCK
    tok = jnp.broadcast_to(jnp.arange(t, dtype=jnp.int32)[None, :], (TOP_K, t))

    def layer(x2d, l):
        lidx = l.reshape(1)
        views = _in_proj(x2d, lidx, p["ln_mix"], p["w_in"], p["head_gains"], consts["blockdiag"], consts["rope"], seq)
        merged = None
        for n, dil in enumerate(DILATIONS):
            merged = _band_attn(views[n], dil, b, seq, merged, n == len(DILATIONS) - 1)
        y_d = merged[0]
        bias_tab = _na_bias_table(lax.dynamic_index_in_dim(p["rpb_na"], l, 0, keepdims=False))
        y_n = _na_attn(views[0].reshape(b, seq, 3 * d), bias_tab).reshape(t, W_NA)
        x2d = _out_proj(y_d, y_n, x2d, lidx, p["out_norm_dil"], p["out_norm_na"], p["w_out"], seq)

        h, eid, rank, gate, counts = _router(x2d, lidx, p["ln_ffn"], p["w_route"], p["b_route"], consts["tri"])
        counts = counts[:, 0].astype(jnp.int32)
        padded = (counts + MOE_BLOCK - 1) // MOE_BLOCK * MOE_BLOCK
        pend = jnp.cumsum(padded)
        pstart = pend - padded
        experts = jnp.arange(N_EXPERTS, dtype=jnp.int32)[:, None, None]
        dest = jnp.sum(jnp.where(eid[None] == experts, pstart[:, None, None], 0), axis=0) + rank
        block_e = jnp.minimum(jnp.sum(jnp.arange(n_blocks)[:, None] * MOE_BLOCK >= pend[None, :], axis=-1),
                              N_EXPERTS - 1).astype(jnp.int32)
        n_used = (pend[-1:] // MOE_BLOCK).astype(jnp.int32)
        slot_tok = (jnp.arange(n_slots, dtype=jnp.int32) % t).at[dest.reshape(-1)].set(
            tok.reshape(-1), mode="promise_in_bounds", unique_indices=True)
        xs = h.at[slot_tok].get(mode="promise_in_bounds")
        yb = _experts(xs, lidx, block_e, n_used, p["w_gate"], p["w_up"], p["w_down"])
        y = (yb.at[dest[0]].get(mode="promise_in_bounds") * gate[0][:, None]
             + yb.at[dest[1]].get(mode="promise_in_bounds") * gate[1][:, None])
        return x2d + y, None

    x2d, _ = lax.scan(layer, x.reshape(t, d), jnp.arange(DEPTH, dtype=jnp.int32))
    return x2d.reshape(b, seq, d)


def kernel(x_prompt, x_sample, ln_mix, w_in, q_norm_dil, k_norm_dil, q_norm_na, k_norm_na, rpb_na, out_norm_dil,
           out_norm_na, w_out, ln_ffn, w_router_group, b_router_group, w_router_expert, b_router_expert, w_gate,
           w_up, w_down):
    assert x_prompt.shape[1:] == x_sample.shape[1:]
    seq = x_prompt.shape[1]
    assert seq % (max(DILATIONS) * BAND_SIDE) == 0 and seq % (NA_ROWS_STEP * GRID_W) == 0
    depth = ln_mix.shape[0]
    heads_per_group = W_DIL // HEAD_DIM
    head_gains = jnp.stack([jnp.tile(g, (1, heads_per_group)) for g in (q_norm_dil, k_norm_dil, q_norm_na, k_norm_na)],
                           axis=1)
    w_exp = w_router_expert.transpose(0, 2, 1, 3).reshape(depth, D_MODEL, N_EXPERTS)
    w_route = jnp.concatenate([w_exp, w_router_group], axis=-1)
    w_route = jnp.pad(w_route, ((0, 0), (0, 0), (0, LANES - w_route.shape[-1])))
    b_route = jnp.concatenate([b_router_expert.reshape(depth, N_EXPERTS), b_router_group], axis=-1)
    b_route = jnp.pad(b_route, ((0, 0), (0, LANES - b_route.shape[-1])))[:, None, :]
    params = {
        "ln_mix": ln_mix[:, None, :],
        "w_in": w_in.astype(MXU_DTYPE),
        "head_gains": head_gains,
        "rpb_na": rpb_na,
        "out_norm_dil": out_norm_dil[:, None, :],
        "out_norm_na": out_norm_na[:, None, :],
        "w_out": w_out.astype(MXU_DTYPE),
        "ln_ffn": ln_ffn[:, None, :],
        "w_route": _split_hi_lo(w_route),
        "b_route": b_route,
        "w_gate": w_gate.astype(MXU_DTYPE),
        "w_up": w_up.astype(MXU_DTYPE),
        "w_down": w_down.astype(MXU_DTYPE),
    }
    head_of = np.arange(W_DIL) // HEAD_DIM
    tm_route = min(TM_ROUTE, (x_prompt.shape[0] + x_sample.shape[0]) * seq)
    consts = {
        "blockdiag": jnp.asarray((head_of[:, None] == head_of[None, :]) / HEAD_DIM, MXU_DTYPE),
        "rope": _rope_table(seq),
        "tri": jnp.asarray(np.arange(tm_route)[:, None] < np.arange(tm_route)[None, :], MXU_DTYPE),
    }
    n_prompt = x_prompt.shape[0]
    y = _trunk(jnp.concatenate([x_prompt, x_sample], axis=0), params, consts)
    return y[:n_prompt], y[n_prompt:]
```

```python
import functools

import numpy as np
import jax
import jax.numpy as jnp
from jax import lax
from jax.experimental import pallas as pl
from jax.experimental.pallas import tpu as pltpu

D_MODEL = 1024
DEPTH = 4
HEAD_DIM = 64
N_HEADS_DIL = 8
N_HEADS_NA = 8
W_DIL = N_HEADS_DIL * HEAD_DIM
W_NA = N_HEADS_NA * HEAD_DIM
QKV_DIL = 3 * W_DIL
DILATIONS = (1, 4, 16)
BAND_SIDE = 64
ROT_DIM = HEAD_DIM // 4
ROPE_THETA = 500000.0
GRID_W = 64
NB_ROWS = 8
NB_COLS = 16
N_GROUPS = 4
EXPERTS_PER_GROUP = 8
N_EXPERTS = N_GROUPS * EXPERTS_PER_GROUP
TOP_K = 2
D_EXPERT = D_MODEL // 2
EPS = 1e-6
NEG = -1e30

LANES = 128
MXU_TILE = 256
LOG2E = 1.4426950408889634
MXU_DTYPE = jnp.bfloat16
VMEM_LIMIT = 48 * 1024 * 1024

TM_PROJ = 512
BAND_TILE = {1: 512, 4: 256, 16: 64}
SUB_BAND = 128
BAND_SKEW = {1: (3, 1), 4: (3, 1), 16: (6, 2)}
NA_ROWS_STEP = 8
NA_ROWS_ITER = 2
NA_SKEW = (3, 1)
TM_ROUTE = 512
MOE_BLOCK = 512


def _cparams(*sem):
    return pltpu.CompilerParams(dimension_semantics=sem, vmem_limit_bytes=VMEM_LIMIT)


def _in_proj_kernel(l_ref, x_ref, g_ref, w_ref, hg_ref, bd_ref, rope_ref, o_ref, *rest):
    del l_ref
    class_refs, stage_ref = rest[:-1], rest[-1]
    tm = x_ref.shape[0]
    x = x_ref[...]
    ms = jnp.mean(x * x, axis=-1, keepdims=True)
    h = (x * lax.rsqrt(ms + EPS) * g_ref[...]).astype(MXU_DTYPE)
    cos = rope_ref[:, 0:LANES]
    sin_up = rope_ref[:, LANES:2 * LANES]
    sin_dn = rope_ref[:, 2 * LANES:3 * LANES]
    half = ROT_DIM // 2
    for c in range(6):
        acc = jnp.dot(h, w_ref[:, c * W_DIL:(c + 1) * W_DIL], preferred_element_type=jnp.float32)
        if c in (0, 1, 3, 4):
            gi = (0, 1, None, 2, 3)[c]
            sq = (acc * acc).astype(MXU_DTYPE)
            bw = bd_ref.shape[0]
            msh = jnp.concatenate(
                [jnp.dot(sq[:, j * bw:(j + 1) * bw], bd_ref[...], preferred_element_type=jnp.float32)
                 for j in range(W_DIL // bw)], axis=1)
            acc = acc * lax.rsqrt(msh + EPS) * hg_ref[gi:gi + 1, :]
        if c in (0, 1):
            parts = []
            for j in range(W_DIL // LANES):
                t = acc[:, j * LANES:(j + 1) * LANES]
                parts.append(t * cos + pltpu.roll(t, LANES - half, 1) * sin_up + pltpu.roll(t, half, 1) * sin_dn)
            acc = jnp.concatenate(parts, axis=1)
        if c in (0, 3):
            acc = acc * (HEAD_DIM ** -0.5 * LOG2E)
        o_ref[:, c * W_DIL:(c + 1) * W_DIL] = acc.astype(o_ref.dtype)
        if c < 3:
            for j in range(W_DIL // LANES):
                stage_ref[j] = acc[:, j * LANES:(j + 1) * LANES]
            for dil, cls_ref in zip(DILATIONS[1:], class_refs):
                for r in range(dil):
                    for j in range(W_DIL // LANES):
                        col = r * QKV_DIL + c * W_DIL + j * LANES
                        rows = stage_ref[j, pl.ds(r, tm // dil, stride=dil), :]
                        cls_ref[:, col:col + LANES] = rows.astype(cls_ref.dtype)


def _in_proj(x2d, lidx, ln_g, w_in, head_gains, blockdiag, rope_tab, seq):
    t, d = x2d.shape
    tm = min(TM_PROJ, seq)
    n_seq_blocks = seq // tm
    class_dils = DILATIONS[1:]
    grid_spec = pltpu.PrefetchScalarGridSpec(
        num_scalar_prefetch=1,
        grid=(t // tm,),
        in_specs=[
            pl.BlockSpec((tm, d), lambda i, l: (i, 0)),
            pl.BlockSpec((None, 1, d), lambda i, l: (l[0], 0, 0)),
            pl.BlockSpec((None, d, 3 * d), lambda i, l: (l[0], 0, 0)),
            pl.BlockSpec((None, 4, W_DIL), lambda i, l: (l[0], 0, 0)),
            pl.BlockSpec((MXU_TILE, MXU_TILE), lambda i, l: (0, 0)),
            pl.BlockSpec((tm, 3 * LANES), lambda i, l: (i % n_seq_blocks, 0)),
        ],
        out_specs=[pl.BlockSpec((tm, 3 * d), lambda i, l: (i, 0))]
        + [pl.BlockSpec((tm // dil, dil * QKV_DIL), lambda i, l: (i, 0)) for dil in class_dils],
        scratch_shapes=[pltpu.VMEM((W_DIL // LANES, tm, LANES), jnp.float32)],
    )
    return pl.pallas_call(
        _in_proj_kernel,
        grid_spec=grid_spec,
        out_shape=[jax.ShapeDtypeStruct((t, 3 * d), MXU_DTYPE)]
        + [jax.ShapeDtypeStruct((t // dil, dil * QKV_DIL), MXU_DTYPE) for dil in class_dils],
        compiler_params=_cparams("parallel"),
        name="in_proj",
    )(lidx, x2d, ln_g, w_in, head_gains, blockdiag, rope_tab)


def _pair_rows(x, low_mask):
    zero = jnp.zeros_like(x)
    return jnp.concatenate([jnp.where(low_mask, x, zero), jnp.where(low_mask, zero, x)], axis=0)


def _softmax_pv(s, v):
    m = jnp.max(s, axis=-1, keepdims=True)
    p = jnp.exp2(s - m)
    den = jnp.sum(p, axis=-1, keepdims=True)
    return jnp.dot(p.astype(MXU_DTYPE), v, preferred_element_type=jnp.float32), den, m


def _staged(work, scores, attend, emit, ahead, lag):
    s_queue, a_queue = [], []
    for step in range(len(work) + ahead + lag):
        if step < len(work):
            s_queue.append(scores(*work[step]))
        if 0 <= step - ahead < len(work):
            a_queue.append(attend(s_queue.pop(0), *work[step - ahead]))
        if 0 <= step - ahead - lag < len(work):
            emit(a_queue.pop(0), *work[step - ahead - lag])


def _band_kernel(*refs, dil, tq, sub, n_l, has_prev, is_last):
    cur_ref, before_ref, after_ref, band_ref = refs[:4]
    pos = 4
    if has_prev:
        po_ref, pl_ref = refs[pos:pos + 2]
        pos += 2
    o_ref = refs[pos]
    pos += 1
    if not is_last:
        lse_ref = refs[pos]

    li = pl.program_id(1)
    side = BAND_SIDE
    nk = sub + 2 * side
    low = lax.broadcasted_iota(jnp.int32, (1, LANES), 1) < HEAD_DIM
    col = lax.broadcasted_iota(jnp.int32, (1, nk), 1)
    n_sub = tq // sub

    def window(j, c0):
        cs = slice(c0, c0 + LANES)
        lo, hi = j * sub - side, (j + 1) * sub + side
        parts = [before_ref[:, cs]] if lo < 0 else []
        parts.append(cur_ref[max(lo, 0):min(hi, tq), cs])
        if hi > tq:
            parts.append(after_ref[:, cs])
        return jnp.concatenate(parts, axis=0)

    def scores(r, hp, j):
        q0 = r * QKV_DIL + hp * LANES
        q2 = _pair_rows(cur_ref[j * sub:(j + 1) * sub, q0:q0 + LANES], low)
        s = lax.dot_general(q2, window(j, q0 + W_DIL), (((1,), (1,)), ((), ())), preferred_element_type=jnp.float32)
        s = s + band_ref[...]
        if j == 0:
            s = jnp.where((col >= side) | (li > 0), s, NEG)
        if j == n_sub - 1:
            s = jnp.where((col < nk - side) | (li < n_l - 1), s, NEG)
        return s

    def attend(s, r, hp, j):
        return _softmax_pv(s, window(j, r * QKV_DIL + 2 * W_DIL + hp * LANES))

    def emit(res, r, hp, j):
        o2, den, m = res
        o2 = o2 * (1.0 / den)
        lse2 = m + jnp.log2(den)
        o = jnp.where(low, o2[:sub], o2[sub:])
        lse = jnp.where(low, lse2[:sub], lse2[sub:])
        if dil == 1:
            rows = slice(j * sub, (j + 1) * sub)
        else:
            rows = pl.ds(dil * j * sub + r, sub, stride=dil)
        if has_prev:
            o_a = po_ref[hp, rows, :]
            lse_a = pl_ref[hp, rows, :]
            mx = jnp.maximum(lse_a, lse)
            w_a = jnp.exp2(lse_a - mx)
            w_b = jnp.exp2(lse - mx)
            tot = w_a + w_b
            o = (w_a * o_a + w_b * o) * (1.0 / tot)
            lse = mx + jnp.log2(tot)
        o_ref[hp, rows, :] = o
        if not is_last:
            lse_ref[hp, rows, :] = lse

    work = [(r, hp, j) for r in range(dil) for hp in range(W_DIL // LANES) for j in range(n_sub)]
    ahead, lag = BAND_SKEW[dil]
    _staged(work, scores, attend, emit, ahead, lag)


def _band_attn(cls, dil, batch, seq, prev, is_last):
    l_len = seq // dil
    tq = min(BAND_TILE[dil], l_len)
    sub = min(SUB_BAND, tq)
    n_l = l_len // tq
    halo_per_tile = tq // BAND_SIDE
    n_halo = l_len // BAND_SIDE
    width = dil * QKV_DIL

    cur = pl.BlockSpec((tq, width), lambda bi, l: (bi * n_l + l, 0))
    before = pl.BlockSpec((BAND_SIDE, width),
                          lambda bi, l: (bi * n_halo + jnp.maximum(l * halo_per_tile - 1, 0), 0))
    after = pl.BlockSpec((BAND_SIDE, width),
                         lambda bi, l: (bi * n_halo + jnp.minimum((l + 1) * halo_per_tile, n_halo - 1), 0))
    n_pairs = W_DIL // LANES
    nat = pl.BlockSpec((n_pairs, dil * tq, LANES), lambda bi, l: (0, bi * n_l + l, 0))
    band = _band_bias(sub)
    in_specs = [cur, before, after, pl.BlockSpec(band.shape, lambda bi, l: (0, 0))]
    args = [cls, cls, cls, band]
    has_prev = prev is not None
    if has_prev:
        in_specs += [nat, nat]
        args += list(prev)
    o_shape = jax.ShapeDtypeStruct((n_pairs, batch * seq, LANES), jnp.float32)
    n_out = 1 if is_last else 2
    return pl.pallas_call(
        functools.partial(_band_kernel, dil=dil, tq=tq, sub=sub, n_l=n_l, has_prev=has_prev, is_last=is_last),
        grid=(batch, n_l),
        in_specs=in_specs,
        out_specs=[nat] * n_out,
        out_shape=[o_shape] * n_out,
        compiler_params=_cparams("parallel", "parallel"),
        name=f"band_attn_d{dil}",
    )(*args)


def _band_bias(sub):
    nk = sub + 2 * BAND_SIDE
    r = np.arange(2 * sub)[:, None] % sub
    c = np.arange(nk)[None, :]
    ok = (c - r >= 0) & (c - r <= 2 * BAND_SIDE)
    return jnp.asarray(np.where(ok, 0.0, NEG), jnp.float32)


def _na_kernel(q_ref, kc_ref, kp_ref, kn_ref, vc_ref, vp_ref, vn_ref, bias_ref, o_ref, kk_ref, vv_ref, *, n_rows):
    step = pl.program_id(1)
    halo = (NB_ROWS // 2) * GRID_W
    cur = NA_ROWS_STEP * GRID_W
    kk_ref[0:halo, :] = kp_ref[...]
    kk_ref[halo:halo + cur, :] = kc_ref[...]
    kk_ref[halo + cur:, :] = kn_ref[...]
    vv_ref[0:halo, :] = vp_ref[...]
    vv_ref[halo:halo + cur, :] = vc_ref[...]
    vv_ref[halo + cur:, :] = vn_ref[...]

    low = lax.broadcasted_iota(jnp.int32, (1, LANES), 1) < HEAD_DIM
    n_keys = NB_ROWS * GRID_W
    row_base = step * NA_ROWS_STEP

    def rows_iter(it, carry):
        work = []
        for u in range(NA_ROWS_ITER):
            i = it * NA_ROWS_ITER + u
            r = row_base + i
            r0 = jnp.clip(r - NB_ROWS // 2, 0, n_rows - NB_ROWS)
            start = pl.multiple_of((r0 - row_base + NB_ROWS // 2) * GRID_W, GRID_W)
            qrow = pl.multiple_of(i * GRID_W, GRID_W)
            work += [(hp, qrow, start, r - r0) for hp in range(W_NA // LANES)]

        def scores(hp, qrow, start, variant):
            cs = slice(hp * LANES, (hp + 1) * LANES)
            q2 = _pair_rows(q_ref[pl.ds(qrow, GRID_W), cs], low)
            keys = kk_ref[pl.ds(start, n_keys), cs]
            s = lax.dot_general(q2, keys, (((1,), (1,)), ((), ())), preferred_element_type=jnp.float32)
            return s + bias_ref[hp, variant]

        def attend(s, hp, qrow, start, variant):
            return _softmax_pv(s, vv_ref[pl.ds(start, n_keys), hp * LANES:(hp + 1) * LANES])

        def emit(res, hp, qrow, start, variant):
            o2, den, _ = res
            o2 = o2 * (1.0 / den)
            o_ref[pl.ds(qrow, GRID_W), hp * LANES:(hp + 1) * LANES] = jnp.where(low, o2[:GRID_W], o2[GRID_W:])

        _staged(work, scores, attend, emit, *NA_SKEW)
        return carry

    lax.fori_loop(0, NA_ROWS_STEP // NA_ROWS_ITER, rows_iter, 0)


def _na_attn(proj, bias_tab):
    b, seq, width = proj.shape
    n_rows = seq // GRID_W
    cur = NA_ROWS_STEP * GRID_W
    halo = (NB_ROWS // 2) * GRID_W
    halo_per_step = cur // halo
    n_halo = seq // halo
    n_pairs = W_NA // LANES

    def cur_spec(which):
        return pl.BlockSpec((None, cur, W_NA), lambda bi, i: (bi, i, which))

    def before(which):
        return pl.BlockSpec((None, halo, W_NA), lambda bi, i: (bi, jnp.maximum(i * halo_per_step - 1, 0), which))

    def after(which):
        return pl.BlockSpec((None, halo, W_NA),
                            lambda bi, i: (bi, jnp.minimum((i + 1) * halo_per_step, n_halo - 1), which))

    return pl.pallas_call(
        functools.partial(_na_kernel, n_rows=n_rows),
        grid=(b, n_rows // NA_ROWS_STEP),
        in_specs=[cur_spec(3), cur_spec(4), before(4), after(4), cur_spec(5), before(5), after(5),
                  pl.BlockSpec((n_pairs, NB_ROWS, 2 * GRID_W, NB_ROWS * GRID_W), lambda bi, i: (0, 0, 0, 0))],
        out_specs=pl.BlockSpec((None, cur, W_NA), lambda bi, i: (bi, i, 0)),
        out_shape=jax.ShapeDtypeStruct((b, seq, W_NA), jnp.float32),
        scratch_shapes=[pltpu.VMEM((cur + 2 * halo, W_NA), MXU_DTYPE),
                        pltpu.VMEM((cur + 2 * halo, W_NA), MXU_DTYPE)],
        compiler_params=_cparams("parallel", "parallel"),
        name="na_attn",
    )(proj, proj, proj, proj, proj, proj, proj, bias_tab)


def _na_bias_table(rpb):
    c = np.arange(GRID_W)
    wstart = np.clip(c - NB_COLS // 2, 0, GRID_W - NB_COLS)
    kc = np.arange(GRID_W)
    valid = (kc[None, :] >= wstart[:, None]) & (kc[None, :] < wstart[:, None] + NB_COLS)
    dc = np.clip(kc[None, :] - c[:, None] + NB_COLS - 1, 0, 2 * NB_COLS - 2)
    dr = np.arange(NB_ROWS)[None, :] - np.arange(NB_ROWS)[:, None] + NB_ROWS - 1
    tab = rpb.astype(jnp.float32)[:, dr][:, :, :, dc]
    tab = jnp.where(jnp.asarray(valid)[None, None, None], tab * LOG2E, NEG)
    tab = tab.transpose(0, 1, 3, 2, 4)
    n_pairs = N_HEADS_NA // 2
    tab = tab.reshape(n_pairs, 2, NB_ROWS, GRID_W, NB_ROWS * GRID_W).transpose(0, 2, 1, 3, 4)
    return tab.reshape(n_pairs, NB_ROWS, 2 * GRID_W, NB_ROWS * GRID_W)


def _out_proj_kernel(l_ref, yd_ref, yn_ref, x_ref, gd_ref, gn_ref, w_ref, o_ref):
    del l_ref

    def norm(y, g):
        ms = jnp.mean(y * y, axis=-1, keepdims=True)
        return (y * lax.rsqrt(ms + EPS) * g).astype(MXU_DTYPE)

    nd = norm(jnp.concatenate([yd_ref[j] for j in range(yd_ref.shape[0])], axis=1), gd_ref[...])
    nn = norm(yn_ref[...], gn_ref[...])
    y = jnp.dot(nd, w_ref[0:W_DIL, :], preferred_element_type=jnp.float32)
    y = y + jnp.dot(nn, w_ref[W_DIL:, :], preferred_element_type=jnp.float32)
    o_ref[...] = x_ref[...] + y


def _out_proj(y_d, y_n, x2d, lidx, g_d, g_n, w_out, seq):
    t, d = x2d.shape
    tm = min(TM_PROJ, seq)
    grid_spec = pltpu.PrefetchScalarGridSpec(
        num_scalar_prefetch=1,
        grid=(t // tm,),
        in_specs=[
            pl.BlockSpec((W_DIL // LANES, tm, LANES), lambda i, l: (0, i, 0)),
            pl.BlockSpec((tm, W_NA), lambda i, l: (i, 0)),
            pl.BlockSpec((tm, d), lambda i, l: (i, 0)),
            pl.BlockSpec((None, 1, W_DIL), lambda i, l: (l[0], 0, 0)),
            pl.BlockSpec((None, 1, W_NA), lambda i, l: (l[0], 0, 0)),
            pl.BlockSpec((None, d, d), lambda i, l: (l[0], 0, 0)),
        ],
        out_specs=pl.BlockSpec((tm, d), lambda i, l: (i, 0)),
    )
    return pl.pallas_call(
        _out_proj_kernel,
        grid_spec=grid_spec,
        out_shape=jax.ShapeDtypeStruct((t, d), jnp.float32),
        compiler_params=_cparams("parallel"),
        name="out_proj",
    )(lidx, y_d, y_n, x2d, g_d, g_n, w_out)


def _router_kernel(l_ref, x_ref, g_ref, w_ref, b_ref, tri_ref, h_ref, eid_ref, rank_ref, gate_ref, cnt_ref, carry_ref):
    del l_ref

    @pl.when(pl.program_id(0) == 0)
    def _():
        carry_ref[...] = jnp.zeros_like(carry_ref)

    x = x_ref[...]
    ms = jnp.mean(x * x, axis=-1, keepdims=True)
    h = x * lax.rsqrt(ms + EPS) * g_ref[...]
    h_hi = h.astype(MXU_DTYPE)
    h_ref[...] = h_hi
    h_lo = (h - h_hi.astype(jnp.float32)).astype(MXU_DTYPE)
    both = jnp.dot(h_hi, w_ref[...], preferred_element_type=jnp.float32)
    logits = both[:, :LANES] + both[:, LANES:]
    logits = logits + jnp.dot(h_lo, w_ref[:, :LANES], preferred_element_type=jnp.float32) + b_ref[...]
    lt = logits.T
    tm = lt.shape[1]

    best = lt[N_EXPERTS:N_EXPERTS + 1]
    gsel = jnp.zeros_like(best)
    for g in range(1, N_GROUPS):
        cand = lt[N_EXPERTS + g:N_EXPERTS + g + 1]
        upd = cand > best
        gsel = jnp.where(upd, float(g), gsel)
        best = jnp.where(upd, cand, best)
    den = jnp.zeros_like(best)
    for g in range(N_GROUPS):
        den = den + jnp.exp(lt[N_EXPERTS + g:N_EXPERTS + g + 1] - best)
    g_gate = 1.0 / den

    e8 = EXPERTS_PER_GROUP
    sel = lt[0:e8]
    for g in range(1, N_GROUPS):
        sel = jnp.where(gsel == float(g), lt[g * e8:(g + 1) * e8], sel)
    row = lax.broadcasted_iota(jnp.int32, (e8, tm), 0).astype(jnp.float32)
    v1 = jnp.max(sel, axis=0, keepdims=True)
    i1 = jnp.min(jnp.where(sel == v1, row, float(e8)), axis=0, keepdims=True)
    rest = jnp.where(row == i1, -jnp.inf, sel)
    v2 = jnp.max(rest, axis=0, keepdims=True)
    i2 = jnp.min(jnp.where(rest == v2, row, float(e8)), axis=0, keepdims=True)
    e2 = jnp.exp(v2 - v1)
    inv = 1.0 / (1.0 + e2)
    gate_ref[0:1, :] = g_gate * inv
    gate_ref[1:2, :] = g_gate * (e2 * inv)
    eid1 = gsel * float(e8) + i1
    eid2 = gsel * float(e8) + i2
    eid_ref[0:1, :] = eid1.astype(jnp.int32)
    eid_ref[1:2, :] = eid2.astype(jnp.int32)

    erow = lax.broadcasted_iota(jnp.int32, (N_EXPERTS, tm), 0).astype(jnp.float32)
    hit1 = erow == eid1
    hit2 = erow == eid2
    oh = jnp.concatenate([hit1, hit2], axis=0).astype(jnp.float32)
    pref = jnp.dot(oh.astype(MXU_DTYPE), tri_ref[...], preferred_element_type=jnp.float32)
    tot = jnp.sum(oh, axis=1, keepdims=True)
    carry = carry_ref[:, 0:1]
    val1 = carry + pref[:N_EXPERTS]
    val2 = carry + tot[:N_EXPERTS] + pref[N_EXPERTS:]
    rank_ref[0:1, :] = jnp.sum(jnp.where(hit1, val1, 0.0), axis=0, keepdims=True).astype(jnp.int32)
    rank_ref[1:2, :] = jnp.sum(jnp.where(hit2, val2, 0.0), axis=0, keepdims=True).astype(jnp.int32)
    new_carry = carry_ref[...] + (tot[:N_EXPERTS] + tot[N_EXPERTS:])
    carry_ref[...] = new_carry
    cnt_ref[...] = new_carry


def _router(x2d, lidx, ln_g, w_route, b_route, tri):
    t, d = x2d.shape
    tm = min(TM_ROUTE, t)
    grid_spec = pltpu.PrefetchScalarGridSpec(
        num_scalar_prefetch=1,
        grid=(t // tm,),
        in_specs=[
            pl.BlockSpec((tm, d), lambda i, l: (i, 0)),
            pl.BlockSpec((None, 1, d), lambda i, l: (l[0], 0, 0)),
            pl.BlockSpec((None, d, 2 * LANES), lambda i, l: (l[0], 0, 0)),
            pl.BlockSpec((None, 1, LANES), lambda i, l: (l[0], 0, 0)),
            pl.BlockSpec((tm, tm), lambda i, l: (0, 0)),
        ],
        out_specs=[
            pl.BlockSpec((tm, d), lambda i, l: (i, 0)),
            pl.BlockSpec((TOP_K, tm), lambda i, l: (0, i)),
            pl.BlockSpec((TOP_K, tm), lambda i, l: (0, i)),
            pl.BlockSpec((TOP_K, tm), lambda i, l: (0, i)),
            pl.BlockSpec((N_EXPERTS, LANES), lambda i, l: (0, 0)),
        ],
        scratch_shapes=[pltpu.VMEM((N_EXPERTS, LANES), jnp.float32)],
    )
    return pl.pallas_call(
        _router_kernel,
        grid_spec=grid_spec,
        out_shape=[
            jax.ShapeDtypeStruct((t, d), MXU_DTYPE),
            jax.ShapeDtypeStruct((TOP_K, t), jnp.int32),
            jax.ShapeDtypeStruct((TOP_K, t), jnp.int32),
            jax.ShapeDtypeStruct((TOP_K, t), jnp.float32),
            jax.ShapeDtypeStruct((N_EXPERTS, LANES), jnp.float32),
        ],
        compiler_params=_cparams("arbitrary"),
        name="router",
    )(lidx, x2d, ln_g, w_route, b_route, tri)


def _expert_kernel(l_ref, be_ref, nu_ref, x_ref, wg_ref, wu_ref, wd_ref, o_ref):
    del l_ref, be_ref

    @pl.when(pl.program_id(0) < nu_ref[0])
    def _():
        x = x_ref[...]
        g = jnp.dot(x, wg_ref[...], preferred_element_type=jnp.float32)
        u = jnp.dot(x, wu_ref[...], preferred_element_type=jnp.float32)
        a = (g * jax.nn.sigmoid(g) * u).astype(MXU_DTYPE)
        o_ref[...] = jnp.dot(a, wd_ref[...], preferred_element_type=jnp.float32).astype(o_ref.dtype)

    @pl.when(pl.program_id(0) >= nu_ref[0])
    def _():
        o_ref[...] = jnp.zeros_like(o_ref)


def _experts(xs, lidx, block_e, n_used, w_gate, w_up, w_down):
    n_slots, d = xs.shape
    n_blocks = n_slots // MOE_BLOCK
    grid_spec = pltpu.PrefetchScalarGridSpec(
        num_scalar_prefetch=3,
        grid=(n_blocks,),
        in_specs=[
            pl.BlockSpec((MOE_BLOCK, d), lambda i, l, be, nu: (i, 0)),
            pl.BlockSpec((None, None, d, D_EXPERT), lambda i, l, be, nu: (l[0], be[i], 0, 0)),
            pl.BlockSpec((None, None, d, D_EXPERT), lambda i, l, be, nu: (l[0], be[i], 0, 0)),
            pl.BlockSpec((None, None, D_EXPERT, d), lambda i, l, be, nu: (l[0], be[i], 0, 0)),
        ],
        out_specs=pl.BlockSpec((MOE_BLOCK, d), lambda i, l, be, nu: (i, 0)),
    )
    return pl.pallas_call(
        _expert_kernel,
        grid_spec=grid_spec,
        out_shape=jax.ShapeDtypeStruct((n_slots, d), MXU_DTYPE),
        compiler_params=_cparams("parallel"),
        name="experts",
    )(lidx, block_e, n_used, xs, w_gate, w_up, w_down)


def _rope_table(seq):
    half = ROT_DIM // 2
    inv = ROPE_THETA ** (-jnp.arange(half, dtype=jnp.float32) / half)
    ang = jnp.arange(seq).astype(jnp.float32)[:, None] * inv[None, :]
    cos, sin = jnp.cos(ang), jnp.sin(ang)
    ones = jnp.ones((seq, HEAD_DIM - ROT_DIM), jnp.float32)
    zeros = jnp.zeros((seq, HEAD_DIM - ROT_DIM), jnp.float32)
    zh = jnp.zeros((seq, half), jnp.float32)
    c = jnp.concatenate([cos, cos, ones], axis=1)
    s_up = jnp.concatenate([-sin, zh, zeros], axis=1)
    s_dn = jnp.concatenate([zh, sin, zeros], axis=1)
    rep = LANES // HEAD_DIM
    return jnp.concatenate([jnp.tile(c, (1, rep)), jnp.tile(s_up, (1, rep)), jnp.tile(s_dn, (1, rep))], axis=1)


def _split_hi_lo(w):
    hi = w.astype(MXU_DTYPE)
    lo = (w - hi.astype(jnp.float32)).astype(MXU_DTYPE)
    return jnp.concatenate([hi, lo], axis=-1)


def _trunk(x, p, consts):
    b, seq, d = x.shape
    t = b * seq
    n_assign = t * TOP_K
    n_blocks = -(-n_assign // MOE_BLOCK) + N_EXPERTS
    n_slots = n_blocks * MOE_BLOCK
    tok = jnp.broadcast_to(jnp.arange(t, dtype=jnp.int32)[None, :], (TOP_K, t))

    def layer(x2d, l):
        lidx = l.reshape(1)
        views = _in_proj(x2d, lidx, p["ln_mix"], p["w_in"], p["head_gains"], consts["blockdiag"], consts["rope"], seq)
        merged = None
        for n, dil in enumerate(DILATIONS):
            merged = _band_attn(views[n], dil, b, seq, merged, n == len(DILATIONS) - 1)
        y_d = merged[0]
        bias_tab = _na_bias_table(lax.dynamic_index_in_dim(p["rpb_na"], l, 0, keepdims=False))
        y_n = _na_attn(views[0].reshape(b, seq, 3 * d), bias_tab).reshape(t, W_NA)
        x2d = _out_proj(y_d, y_n, x2d, lidx, p["out_norm_dil"], p["out_norm_na"], p["w_out"], seq)

        h, eid, rank, gate, counts = _router(x2d, lidx, p["ln_ffn"], p["w_route"], p["b_route"], consts["tri"])
        counts = counts[:, 0].astype(jnp.int32)
        padded = (counts + MOE_BLOCK - 1) // MOE_BLOCK * MOE_BLOCK
        pend = jnp.cumsum(padded)
        pstart = pend - padded
        experts = jnp.arange(N_EXPERTS, dtype=jnp.int32)[:, None, None]
        dest = jnp.sum(jnp.where(eid[None] == experts, pstart[:, None, None], 0), axis=0) + rank
        block_e = jnp.minimum(jnp.sum(jnp.arange(n_blocks)[:, None] * MOE_BLOCK >= pend[None, :], axis=-1),
                              N_EXPERTS - 1).astype(jnp.int32)
        n_used = (pend[-1:] // MOE_BLOCK).astype(jnp.int32)
        slot_tok = (jnp.arange(n_slots, dtype=jnp.int32) % t).at[dest.reshape(-1)].set(
            tok.reshape(-1), mode="promise_in_bounds", unique_indices=True)
        xs = h.at[slot_tok].get(mode="promise_in_bounds")
        yb = _experts(xs, lidx, block_e, n_used, p["w_gate"], p["w_up"], p["w_down"])
        y = (yb.at[dest[0]].get(mode="promise_in_bounds").astype(jnp.float32) * gate[0][:, None]
             + yb.at[dest[1]].get(mode="promise_in_bounds").astype(jnp.float32) * gate[1][:, None])
        return x2d + y, None

    x2d, _ = lax.scan(layer, x.reshape(t, d), jnp.arange(DEPTH, dtype=jnp.int32))
    return x2d.reshape(b, seq, d)


def kernel(x_prompt, x_sample, ln_mix, w_in, q_norm_dil, k_norm_dil, q_norm_na, k_norm_na, rpb_na, out_norm_dil,
           out_norm_na, w_out, ln_ffn, w_router_group, b_router_group, w_router_expert, b_router_expert, w_gate,
           w_up, w_down):
    assert x_prompt.shape[1:] == x_sample.shape[1:]
    seq = x_prompt.shape[1]
    assert seq % (max(DILATIONS) * BAND_SIDE) == 0 and seq % (NA_ROWS_STEP * GRID_W) == 0
    depth = ln_mix.shape[0]
    heads_per_group = W_DIL // HEAD_DIM
    head_gains = jnp.stack([jnp.tile(g, (1, heads_per_group)) for g in (q_norm_dil, k_norm_dil, q_norm_na, k_norm_na)],
                           axis=1)
    w_exp = w_router_expert.transpose(0, 2, 1, 3).reshape(depth, D_MODEL, N_EXPERTS)
    w_route = jnp.concatenate([w_exp, w_router_group], axis=-1)
    w_route = jnp.pad(w_route, ((0, 0), (0, 0), (0, LANES - w_route.shape[-1])))
    b_route = jnp.concatenate([b_router_expert.reshape(depth, N_EXPERTS), b_router_group], axis=-1)
    b_route = jnp.pad(b_route, ((0, 0), (0, LANES - b_route.shape[-1])))[:, None, :]
    params = {
        "ln_mix": ln_mix[:, None, :],
        "w_in": w_in.astype(MXU_DTYPE),
        "head_gains": head_gains,
        "rpb_na": rpb_na,
        "out_norm_dil": out_norm_dil[:, None, :],
        "out_norm_na": out_norm_na[:, None, :],
        "w_out": w_out.astype(MXU_DTYPE),
        "ln_ffn": ln_ffn[:, None, :],
        "w_route": _split_hi_lo(w_route),
        "b_route": b_route,
        "w_gate": w_gate.astype(MXU_DTYPE),
        "w_up": w_up.astype(MXU_DTYPE),
        "w_down": w_down.astype(MXU_DTYPE),
    }
    head_of = np.arange(MXU_TILE) // HEAD_DIM
    tm_route = min(TM_ROUTE, (x_prompt.shape[0] + x_sample.shape[0]) * seq)
    consts = {
        "blockdiag": jnp.asarray((head_of[:, None] == head_of[None, :]) / HEAD_DIM, MXU_DTYPE),
        "rope": _rope_table(seq),
        "tri": jnp.asarray(np.arange(tm_route)[:, None] < np.arange(tm_route)[None, :], MXU_DTYPE),
    }
    n_prompt = x_prompt.shape[0]
    y = _trunk(jnp.concatenate([x_prompt, x_sample], axis=0), params, consts)
    return y[:n_prompt], y[n_prompt:]
```

```python
import functools

import numpy as np
import jax
import jax.numpy as jnp
from jax import lax
from jax.experimental import pallas as pl
from jax.experimental.pallas import tpu as pltpu

D_MODEL = 1024
DEPTH = 4
HEAD_DIM = 64
N_HEADS_DIL = 8
N_HEADS_NA = 8
W_DIL = N_HEADS_DIL * HEAD_DIM
W_NA = N_HEADS_NA * HEAD_DIM
QKV_DIL = 3 * W_DIL
DILATIONS = (1, 4, 16)
BAND_SIDE = 64
ROT_DIM = HEAD_DIM // 4
ROPE_THETA = 500000.0
GRID_W = 64
NB_ROWS = 8
NB_COLS = 16
N_GROUPS = 4
EXPERTS_PER_GROUP = 8
N_EXPERTS = N_GROUPS * EXPERTS_PER_GROUP
TOP_K = 2
D_EXPERT = D_MODEL // 2
EPS = 1e-6
NEG = -1e30

LANES = 128
MXU_TILE = 256
LOG2E = 1.4426950408889634
MXU_DTYPE = jnp.bfloat16
VMEM_LIMIT = 48 * 1024 * 1024

TM_PROJ = 512
BAND_TILE = {1: 512, 4: 256, 16: 64}
SUB_BAND = 128
BAND_SKEW = {1: (3, 1), 4: (3, 1), 16: (6, 2)}
NA_ROWS_STEP = 8
NA_ROWS_ITER = 2
NA_SKEW = (3, 1)
TM_ROUTE = 512
MOE_BLOCK = 512


def _cparams(*sem):
    return pltpu.CompilerParams(dimension_semantics=sem, vmem_limit_bytes=VMEM_LIMIT)


def _in_proj_kernel(l_ref, x_ref, g_ref, w_ref, hg_ref, bd_ref, rope_ref, o_ref, *rest):
    del l_ref
    class_refs, stage_ref = rest[:-1], rest[-1]
    tm = x_ref.shape[0]
    x = x_ref[...]
    ms = jnp.mean(x * x, axis=-1, keepdims=True)
    h = (x * lax.rsqrt(ms + EPS) * g_ref[...]).astype(MXU_DTYPE)
    cos = rope_ref[:, 0:LANES]
    sin_up = rope_ref[:, LANES:2 * LANES]
    sin_dn = rope_ref[:, 2 * LANES:3 * LANES]
    half = ROT_DIM // 2
    for c in range(6):
        acc = jnp.dot(h, w_ref[:, c * W_DIL:(c + 1) * W_DIL], preferred_element_type=jnp.float32)
        if c in (0, 1, 3, 4):
            gi = (0, 1, None, 2, 3)[c]
            sq = (acc * acc).astype(MXU_DTYPE)
            bw = bd_ref.shape[0]
            msh = jnp.concatenate(
                [jnp.dot(sq[:, j * bw:(j + 1) * bw], bd_ref[...], preferred_element_type=jnp.float32)
                 for j in range(W_DIL // bw)], axis=1)
            acc = acc * lax.rsqrt(msh + EPS) * hg_ref[gi:gi + 1, :]
        if c in (0, 1):
            parts = []
            for j in range(W_DIL // LANES):
                t = acc[:, j * LANES:(j + 1) * LANES]
                parts.append(t * cos + pltpu.roll(t, LANES - half, 1) * sin_up + pltpu.roll(t, half, 1) * sin_dn)
            acc = jnp.concatenate(parts, axis=1)
        if c in (0, 3):
            acc = acc * (HEAD_DIM ** -0.5 * LOG2E)
        o_ref[:, c * W_DIL:(c + 1) * W_DIL] = acc.astype(o_ref.dtype)
        if c < 3:
            for j in range(W_DIL // LANES):
                stage_ref[j] = acc[:, j * LANES:(j + 1) * LANES]
            for dil, cls_ref in zip(DILATIONS[1:], class_refs):
                for r in range(dil):
                    for j in range(W_DIL // LANES):
                        col = r * QKV_DIL + c * W_DIL + j * LANES
                        rows = stage_ref[j, pl.ds(r, tm // dil, stride=dil), :]
                        cls_ref[:, col:col + LANES] = rows.astype(cls_ref.dtype)


def _in_proj(x2d, lidx, ln_g, w_in, head_gains, blockdiag, rope_tab, seq):
    t, d = x2d.shape
    tm = min(TM_PROJ, seq)
    n_seq_blocks = seq // tm
    class_dils = DILATIONS[1:]
    grid_spec = pltpu.PrefetchScalarGridSpec(
        num_scalar_prefetch=1,
        grid=(t // tm,),
        in_specs=[
            pl.BlockSpec((tm, d), lambda i, l: (i, 0)),
            pl.BlockSpec((None, 1, d), lambda i, l: (l[0], 0, 0)),
            pl.BlockSpec((None, d, 3 * d), lambda i, l: (l[0], 0, 0)),
            pl.BlockSpec((None, 4, W_DIL), lambda i, l: (l[0], 0, 0)),
            pl.BlockSpec((MXU_TILE, MXU_TILE), lambda i, l: (0, 0)),
            pl.BlockSpec((tm, 3 * LANES), lambda i, l: (i % n_seq_blocks, 0)),
        ],
        out_specs=[pl.BlockSpec((tm, 3 * d), lambda i, l: (i, 0))]
        + [pl.BlockSpec((tm // dil, dil * QKV_DIL), lambda i, l: (i, 0)) for dil in class_dils],
        scratch_shapes=[pltpu.VMEM((W_DIL // LANES, tm, LANES), jnp.float32)],
    )
    return pl.pallas_call(
        _in_proj_kernel,
        grid_spec=grid_spec,
        out_shape=[jax.ShapeDtypeStruct((t, 3 * d), MXU_DTYPE)]
        + [jax.ShapeDtypeStruct((t // dil, dil * QKV_DIL), MXU_DTYPE) for dil in class_dils],
        compiler_params=_cparams("parallel"),
        name="in_proj",
    )(lidx, x2d, ln_g, w_in, head_gains, blockdiag, rope_tab)


def _pair_rows(x, low_mask):
    zero = jnp.zeros_like(x)
    return jnp.concatenate([jnp.where(low_mask, x, zero), jnp.where(low_mask, zero, x)], axis=0)


def _softmax_pv(s, v):
    m = jnp.max(s, axis=-1, keepdims=True)
    p = jnp.exp2(s - m)
    den = jnp.sum(p, axis=-1, keepdims=True)
    return jnp.dot(p.astype(MXU_DTYPE), v, preferred_element_type=jnp.float32), den, m


def _staged(work, scores, attend, emit, ahead, lag):
    s_queue, a_queue = [], []
    for step in range(len(work) + ahead + lag):
        if step < len(work):
            s_queue.append(scores(*work[step]))
        if 0 <= step - ahead < len(work):
            a_queue.append(attend(s_queue.pop(0), *work[step - ahead]))
        if 0 <= step - ahead - lag < len(work):
            emit(a_queue.pop(0), *work[step - ahead - lag])


def _band_kernel(*refs, dil, tq, sub, n_l, has_prev, is_last):
    cur_ref, before_ref, after_ref, band_ref = refs[:4]
    pos = 4
    if has_prev:
        po_ref, pl_ref = refs[pos:pos + 2]
        pos += 2
    o_ref = refs[pos]
    pos += 1
    if not is_last:
        lse_ref = refs[pos]

    li = pl.program_id(1)
    side = BAND_SIDE
    nk = sub + 2 * side
    low = lax.broadcasted_iota(jnp.int32, (1, LANES), 1) < HEAD_DIM
    col = lax.broadcasted_iota(jnp.int32, (1, nk), 1)
    n_sub = tq // sub

    def window(j, c0):
        cs = slice(c0, c0 + LANES)
        lo, hi = j * sub - side, (j + 1) * sub + side
        parts = [before_ref[:, cs]] if lo < 0 else []
        parts.append(cur_ref[max(lo, 0):min(hi, tq), cs])
        if hi > tq:
            parts.append(after_ref[:, cs])
        return jnp.concatenate(parts, axis=0)

    def scores(r, hp, j):
        q0 = r * QKV_DIL + hp * LANES
        q2 = _pair_rows(cur_ref[j * sub:(j + 1) * sub, q0:q0 + LANES], low)
        s = lax.dot_general(q2, window(j, q0 + W_DIL), (((1,), (1,)), ((), ())), preferred_element_type=jnp.float32)
        s = s + band_ref[...]
        if j == 0:
            s = jnp.where((col >= side) | (li > 0), s, NEG)
        if j == n_sub - 1:
            s = jnp.where((col < nk - side) | (li < n_l - 1), s, NEG)
        return s

    def attend(s, r, hp, j):
        return _softmax_pv(s, window(j, r * QKV_DIL + 2 * W_DIL + hp * LANES))

    def emit(res, r, hp, j):
        o2, den, m = res
        o2 = o2 * (1.0 / den)
        lse2 = m + jnp.log2(den)
        o = jnp.where(low, o2[:sub], o2[sub:])
        lse = jnp.where(low, lse2[:sub], lse2[sub:])
        if dil == 1:
            rows = slice(j * sub, (j + 1) * sub)
        else:
            rows = pl.ds(dil * j * sub + r, sub, stride=dil)
        if has_prev:
            o_a = po_ref[hp, rows, :]
            lse_a = pl_ref[hp, rows, :]
            mx = jnp.maximum(lse_a, lse)
            w_a = jnp.exp2(lse_a - mx)
            w_b = jnp.exp2(lse - mx)
            tot = w_a + w_b
            o = (w_a * o_a + w_b * o) * (1.0 / tot)
            lse = mx + jnp.log2(tot)
        o_ref[hp, rows, :] = o
        if not is_last:
            lse_ref[hp, rows, :] = lse

    work = [(r, hp, j) for r in range(dil) for hp in range(W_DIL // LANES) for j in range(n_sub)]
    ahead, lag = BAND_SKEW[dil]
    _staged(work, scores, attend, emit, ahead, lag)


def _band_attn(cls, dil, batch, seq, prev, is_last):
    l_len = seq // dil
    tq = min(BAND_TILE[dil], l_len)
    sub = min(SUB_BAND, tq)
    n_l = l_len // tq
    halo_per_tile = tq // BAND_SIDE
    n_halo = l_len // BAND_SIDE
    width = dil * QKV_DIL

    cur = pl.BlockSpec((tq, width), lambda bi, l: (bi * n_l + l, 0))
    before = pl.BlockSpec((BAND_SIDE, width),
                          lambda bi, l: (bi * n_halo + jnp.maximum(l * halo_per_tile - 1, 0), 0))
    after = pl.BlockSpec((BAND_SIDE, width),
                         lambda bi, l: (bi * n_halo + jnp.minimum((l + 1) * halo_per_tile, n_halo - 1), 0))
    n_pairs = W_DIL // LANES
    nat = pl.BlockSpec((n_pairs, dil * tq, LANES), lambda bi, l: (0, bi * n_l + l, 0))
    band = _band_bias(sub)
    in_specs = [cur, before, after, pl.BlockSpec(band.shape, lambda bi, l: (0, 0))]
    args = [cls, cls, cls, band]
    has_prev = prev is not None
    if has_prev:
        in_specs += [nat, nat]
        args += list(prev)
    o_shape = jax.ShapeDtypeStruct((n_pairs, batch * seq, LANES), jnp.float32)
    n_out = 1 if is_last else 2
    return pl.pallas_call(
        functools.partial(_band_kernel, dil=dil, tq=tq, sub=sub, n_l=n_l, has_prev=has_prev, is_last=is_last),
        grid=(batch, n_l),
        in_specs=in_specs,
        out_specs=[nat] * n_out,
        out_shape=[o_shape] * n_out,
        compiler_params=_cparams("parallel", "parallel"),
        name=f"band_attn_d{dil}",
    )(*args)


def _band_bias(sub):
    nk = sub + 2 * BAND_SIDE
    r = np.arange(2 * sub)[:, None] % sub
    c = np.arange(nk)[None, :]
    ok = (c - r >= 0) & (c - r <= 2 * BAND_SIDE)
    return jnp.asarray(np.where(ok, 0.0, NEG), jnp.float32)


def _na_kernel(q_ref, kc_ref, kp_ref, kn_ref, vc_ref, vp_ref, vn_ref, bias_ref, o_ref, kk_ref, vv_ref, *, n_rows):
    step = pl.program_id(1)
    halo = (NB_ROWS // 2) * GRID_W
    cur = NA_ROWS_STEP * GRID_W
    kk_ref[0:halo, :] = kp_ref[...]
    kk_ref[halo:halo + cur, :] = kc_ref[...]
    kk_ref[halo + cur:, :] = kn_ref[...]
    vv_ref[0:halo, :] = vp_ref[...]
    vv_ref[halo:halo + cur, :] = vc_ref[...]
    vv_ref[halo + cur:, :] = vn_ref[...]

    low = lax.broadcasted_iota(jnp.int32, (1, LANES), 1) < HEAD_DIM
    n_keys = NB_ROWS * GRID_W
    row_base = step * NA_ROWS_STEP

    def rows_iter(it, carry):
        work = []
        for u in range(NA_ROWS_ITER):
            i = it * NA_ROWS_ITER + u
            r = row_base + i
            r0 = jnp.clip(r - NB_ROWS // 2, 0, n_rows - NB_ROWS)
            start = pl.multiple_of((r0 - row_base + NB_ROWS // 2) * GRID_W, GRID_W)
            qrow = pl.multiple_of(i * GRID_W, GRID_W)
            work += [(hp, qrow, start, r - r0) for hp in range(W_NA // LANES)]

        def scores(hp, qrow, start, variant):
            cs = slice(hp * LANES, (hp + 1) * LANES)
            q2 = _pair_rows(q_ref[pl.ds(qrow, GRID_W), cs], low)
            keys = kk_ref[pl.ds(start, n_keys), cs]
            s = lax.dot_general(q2, keys, (((1,), (1,)), ((), ())), preferred_element_type=jnp.float32)
            return s + bias_ref[hp, variant]

        def attend(s, hp, qrow, start, variant):
            return _softmax_pv(s, vv_ref[pl.ds(start, n_keys), hp * LANES:(hp + 1) * LANES])

        def emit(res, hp, qrow, start, variant):
            o2, den, _ = res
            o2 = o2 * (1.0 / den)
            o_ref[pl.ds(qrow, GRID_W), hp * LANES:(hp + 1) * LANES] = jnp.where(low, o2[:GRID_W], o2[GRID_W:])

        _staged(work, scores, attend, emit, *NA_SKEW)
        return carry

    lax.fori_loop(0, NA_ROWS_STEP // NA_ROWS_ITER, rows_iter, 0)


def _na_attn(proj, bias_tab):
    b, seq, width = proj.shape
    n_rows = seq // GRID_W
    cur = NA_ROWS_STEP * GRID_W
    halo = (NB_ROWS // 2) * GRID_W
    halo_per_step = cur // halo
    n_halo = seq // halo
    n_pairs = W_NA // LANES

    def cur_spec(which):
        return pl.BlockSpec((None, cur, W_NA), lambda bi, i: (bi, i, which))

    def before(which):
        return pl.BlockSpec((None, halo, W_NA), lambda bi, i: (bi, jnp.maximum(i * halo_per_step - 1, 0), which))

    def after(which):
        return pl.BlockSpec((None, halo, W_NA),
                            lambda bi, i: (bi, jnp.minimum((i + 1) * halo_per_step, n_halo - 1), which))

    return pl.pallas_call(
        functools.partial(_na_kernel, n_rows=n_rows),
        grid=(b, n_rows // NA_ROWS_STEP),
        in_specs=[cur_spec(3), cur_spec(4), before(4), after(4), cur_spec(5), before(5), after(5),
                  pl.BlockSpec((n_pairs, NB_ROWS, 2 * GRID_W, NB_ROWS * GRID_W), lambda bi, i: (0, 0, 0, 0))],
        out_specs=pl.BlockSpec((None, cur, W_NA), lambda bi, i: (bi, i, 0)),
        out_shape=jax.ShapeDtypeStruct((b, seq, W_NA), jnp.float32),
        scratch_shapes=[pltpu.VMEM((cur + 2 * halo, W_NA), MXU_DTYPE),
                        pltpu.VMEM((cur + 2 * halo, W_NA), MXU_DTYPE)],
        compiler_params=_cparams("parallel", "parallel"),
        name="na_attn",
    )(proj, proj, proj, proj, proj, proj, proj, bias_tab)


def _na_bias_table(rpb):
    c = np.arange(GRID_W)
    wstart = np.clip(c - NB_COLS // 2, 0, GRID_W - NB_COLS)
    kc = np.arange(GRID_W)
    valid = (kc[None, :] >= wstart[:, None]) & (kc[None, :] < wstart[:, None] + NB_COLS)
    dc = np.clip(kc[None, :] - c[:, None] + NB_COLS - 1, 0, 2 * NB_COLS - 2)
    pick = jnp.asarray(dc[None, :, :] == np.arange(2 * NB_COLS - 1)[:, None, None])
    by_col = jnp.sum(jnp.where(pick[None, None], rpb.astype(jnp.float32)[:, :, :, None, None], 0.0), axis=2)
    by_col = jnp.where(jnp.asarray(valid)[None, None], by_col * LOG2E, NEG)
    tab = jnp.stack([by_col[:, NB_ROWS - 1 - v:2 * NB_ROWS - 1 - v] for v in range(NB_ROWS)], axis=1)
    tab = tab.transpose(0, 1, 3, 2, 4)
    n_pairs = N_HEADS_NA // 2
    tab = tab.reshape(n_pairs, 2, NB_ROWS, GRID_W, NB_ROWS * GRID_W).transpose(0, 2, 1, 3, 4)
    return tab.reshape(n_pairs, NB_ROWS, 2 * GRID_W, NB_ROWS * GRID_W)


def _out_route_kernel(l_ref, yd_ref, yn_ref, x_ref, gd_ref, gn_ref, w_ref, g_ref, wr_ref, b_ref, tri_ref,
                      o_ref, h_ref, eid_ref, rank_ref, gate_ref, cnt_ref, carry_ref):
    del l_ref

    def norm(y, g):
        ms = jnp.mean(y * y, axis=-1, keepdims=True)
        return (y * lax.rsqrt(ms + EPS) * g).astype(MXU_DTYPE)

    nd = norm(jnp.concatenate([yd_ref[j] for j in range(yd_ref.shape[0])], axis=1), gd_ref[...])
    nn = norm(yn_ref[...], gn_ref[...])
    y = jnp.dot(nd, w_ref[0:W_DIL, :], preferred_element_type=jnp.float32)
    y = y + jnp.dot(nn, w_ref[W_DIL:, :], preferred_element_type=jnp.float32)
    x = x_ref[...] + y
    o_ref[...] = x
    _route_tile(x, g_ref, wr_ref, b_ref, tri_ref, h_ref, eid_ref, rank_ref, gate_ref, cnt_ref, carry_ref)


def _route_tile(x, g_ref, w_ref, b_ref, tri_ref, h_ref, eid_ref, rank_ref, gate_ref, cnt_ref, carry_ref):
    @pl.when(pl.program_id(0) == 0)
    def _():
        carry_ref[...] = jnp.zeros_like(carry_ref)

    ms = jnp.mean(x * x, axis=-1, keepdims=True)
    h = x * lax.rsqrt(ms + EPS) * g_ref[...]
    h_hi = h.astype(MXU_DTYPE)
    h_ref[...] = h_hi
    h_lo = (h - h_hi.astype(jnp.float32)).astype(MXU_DTYPE)
    both = jnp.dot(h_hi, w_ref[...], preferred_element_type=jnp.float32)
    logits = both[:, :LANES] + both[:, LANES:]
    logits = logits + jnp.dot(h_lo, w_ref[:, :LANES], preferred_element_type=jnp.float32) + b_ref[...]
    lt = logits.T
    tm = lt.shape[1]

    best = lt[N_EXPERTS:N_EXPERTS + 1]
    gsel = jnp.zeros_like(best)
    for g in range(1, N_GROUPS):
        cand = lt[N_EXPERTS + g:N_EXPERTS + g + 1]
        upd = cand > best
        gsel = jnp.where(upd, float(g), gsel)
        best = jnp.where(upd, cand, best)
    den = jnp.zeros_like(best)
    for g in range(N_GROUPS):
        den = den + jnp.exp(lt[N_EXPERTS + g:N_EXPERTS + g + 1] - best)
    g_gate = 1.0 / den

    e8 = EXPERTS_PER_GROUP
    sel = lt[0:e8]
    for g in range(1, N_GROUPS):
        sel = jnp.where(gsel == float(g), lt[g * e8:(g + 1) * e8], sel)
    row = lax.broadcasted_iota(jnp.int32, (e8, tm), 0).astype(jnp.float32)
    v1 = jnp.max(sel, axis=0, keepdims=True)
    i1 = jnp.min(jnp.where(sel == v1, row, float(e8)), axis=0, keepdims=True)
    rest = jnp.where(row == i1, -jnp.inf, sel)
    v2 = jnp.max(rest, axis=0, keepdims=True)
    i2 = jnp.min(jnp.where(rest == v2, row, float(e8)), axis=0, keepdims=True)
    e2 = jnp.exp(v2 - v1)
    inv = 1.0 / (1.0 + e2)
    gate_ref[0:1, :] = g_gate * inv
    gate_ref[1:2, :] = g_gate * (e2 * inv)
    eid1 = gsel * float(e8) + i1
    eid2 = gsel * float(e8) + i2
    eid_ref[0:1, :] = eid1.astype(jnp.int32)
    eid_ref[1:2, :] = eid2.astype(jnp.int32)

    erow = lax.broadcasted_iota(jnp.int32, (N_EXPERTS, tm), 0).astype(jnp.float32)
    hit1 = erow == eid1
    hit2 = erow == eid2
    oh = jnp.concatenate([hit1, hit2], axis=0).astype(jnp.float32)
    pref = jnp.dot(oh.astype(MXU_DTYPE), tri_ref[...], preferred_element_type=jnp.float32)
    tot = jnp.sum(oh, axis=1, keepdims=True)
    carry = carry_ref[:, 0:1]
    val1 = carry + pref[:N_EXPERTS]
    val2 = carry + tot[:N_EXPERTS] + pref[N_EXPERTS:]
    rank_ref[0:1, :] = jnp.sum(jnp.where(hit1, val1, 0.0), axis=0, keepdims=True).astype(jnp.int32)
    rank_ref[1:2, :] = jnp.sum(jnp.where(hit2, val2, 0.0), axis=0, keepdims=True).astype(jnp.int32)
    new_carry = carry_ref[...] + (tot[:N_EXPERTS] + tot[N_EXPERTS:])
    carry_ref[...] = new_carry
    cnt_ref[...] = new_carry


def _out_proj_route(y_d, y_n, x2d, lidx, g_d, g_n, w_out, ln_g, w_route, b_route, tri):
    t, d = x2d.shape
    tm = tri.shape[0]
    per_tile = lambda i, l: (i, 0)
    per_layer = lambda i, l: (l[0], 0, 0)
    per_token_cols = lambda i, l: (0, i)
    grid_spec = pltpu.PrefetchScalarGridSpec(
        num_scalar_prefetch=1,
        grid=(t // tm,),
        in_specs=[
            pl.BlockSpec((W_DIL // LANES, tm, LANES), lambda i, l: (0, i, 0)),
            pl.BlockSpec((tm, W_NA), per_tile),
            pl.BlockSpec((tm, d), per_tile),
            pl.BlockSpec((None, 1, W_DIL), per_layer),
            pl.BlockSpec((None, 1, W_NA), per_layer),
            pl.BlockSpec((None, d, d), per_layer),
            pl.BlockSpec((None, 1, d), per_layer),
            pl.BlockSpec((None, d, 2 * LANES), per_layer),
            pl.BlockSpec((None, 1, LANES), per_layer),
            pl.BlockSpec((tm, tm), lambda i, l: (0, 0)),
        ],
        out_specs=[
            pl.BlockSpec((tm, d), per_tile),
            pl.BlockSpec((tm, d), per_tile),
            pl.BlockSpec((TOP_K, tm), per_token_cols),
            pl.BlockSpec((TOP_K, tm), per_token_cols),
            pl.BlockSpec((TOP_K, tm), per_token_cols),
            pl.BlockSpec((N_EXPERTS, LANES), lambda i, l: (0, 0)),
        ],
        scratch_shapes=[pltpu.VMEM((N_EXPERTS, LANES), jnp.float32)],
    )
    return pl.pallas_call(
        _out_route_kernel,
        grid_spec=grid_spec,
        out_shape=[
            jax.ShapeDtypeStruct((t, d), jnp.float32),
            jax.ShapeDtypeStruct((t, d), MXU_DTYPE),
            jax.ShapeDtypeStruct((TOP_K, t), jnp.int32),
            jax.ShapeDtypeStruct((TOP_K, t), jnp.int32),
            jax.ShapeDtypeStruct((TOP_K, t), jnp.float32),
            jax.ShapeDtypeStruct((N_EXPERTS, LANES), jnp.float32),
        ],
        compiler_params=_cparams("arbitrary"),
        name="out_proj_route",
    )(lidx, y_d, y_n, x2d, g_d, g_n, w_out, ln_g, w_route, b_route, tri)


def _expert_kernel(l_ref, be_ref, nu_ref, x_ref, wg_ref, wu_ref, wd_ref, o_ref, cg_ref, cu_ref, cd_ref):
    del l_ref
    i = pl.program_id(0)

    @pl.when((i == 0) | (be_ref[i] != be_ref[jnp.maximum(i - 1, 0)]))
    def _():
        cg_ref[...] = wg_ref[...].astype(MXU_DTYPE)
        cu_ref[...] = wu_ref[...].astype(MXU_DTYPE)
        cd_ref[...] = wd_ref[...].astype(MXU_DTYPE)

    @pl.when(i < nu_ref[0])
    def _():
        x = x_ref[...]
        g = jnp.dot(x, cg_ref[...], preferred_element_type=jnp.float32)
        u = jnp.dot(x, cu_ref[...], preferred_element_type=jnp.float32)
        a = (g * jax.nn.sigmoid(g) * u).astype(MXU_DTYPE)
        o_ref[...] = jnp.dot(a, cd_ref[...], preferred_element_type=jnp.float32).astype(o_ref.dtype)

    @pl.when(i >= nu_ref[0])
    def _():
        o_ref[...] = jnp.zeros_like(o_ref)


def _experts(xs, lidx, block_e, n_used, w_gate, w_up, w_down):
    n_slots, d = xs.shape
    n_blocks = n_slots // MOE_BLOCK
    grid_spec = pltpu.PrefetchScalarGridSpec(
        num_scalar_prefetch=3,
        grid=(n_blocks,),
        in_specs=[
            pl.BlockSpec((MOE_BLOCK, d), lambda i, l, be, nu: (i, 0)),
            pl.BlockSpec((None, None, d, D_EXPERT), lambda i, l, be, nu: (l[0], be[i], 0, 0)),
            pl.BlockSpec((None, None, d, D_EXPERT), lambda i, l, be, nu: (l[0], be[i], 0, 0)),
            pl.BlockSpec((None, None, D_EXPERT, d), lambda i, l, be, nu: (l[0], be[i], 0, 0)),
        ],
        out_specs=pl.BlockSpec((MOE_BLOCK, d), lambda i, l, be, nu: (i, 0)),
        scratch_shapes=[pltpu.VMEM((d, D_EXPERT), MXU_DTYPE), pltpu.VMEM((d, D_EXPERT), MXU_DTYPE),
                        pltpu.VMEM((D_EXPERT, d), MXU_DTYPE)],
    )
    return pl.pallas_call(
        _expert_kernel,
        grid_spec=grid_spec,
        out_shape=jax.ShapeDtypeStruct((n_slots, d), MXU_DTYPE),
        compiler_params=_cparams("arbitrary"),
        name="experts",
    )(lidx, block_e, n_used, xs, w_gate, w_up, w_down)


def _combine_kernel(ya_ref, yb_ref, gate_ref, x_ref, o_ref):
    ga = gate_ref[:, 0:1]
    gb = gate_ref[:, 1:2]
    o_ref[...] = x_ref[...] + (ya_ref[...].astype(jnp.float32) * ga + yb_ref[...].astype(jnp.float32) * gb)


def _combine(y_a, y_b, gate_t, x2d, tm):
    t, d = x2d.shape
    row = pl.BlockSpec((tm, d), lambda i: (i, 0))
    return pl.pallas_call(
        _combine_kernel,
        grid=(t // tm,),
        in_specs=[row, row, pl.BlockSpec((tm, TOP_K), lambda i: (i, 0)), row],
        out_specs=row,
        out_shape=jax.ShapeDtypeStruct((t, d), jnp.float32),
        compiler_params=_cparams("parallel"),
        name="combine",
    )(y_a, y_b, gate_t, x2d)


def _rope_table(seq):
    half = ROT_DIM // 2
    inv = ROPE_THETA ** (-jnp.arange(half, dtype=jnp.float32) / half)
    ang = jnp.arange(seq).astype(jnp.float32)[:, None] * inv[None, :]
    cos, sin = jnp.cos(ang), jnp.sin(ang)
    ones = jnp.ones((seq, HEAD_DIM - ROT_DIM), jnp.float32)
    zeros = jnp.zeros((seq, HEAD_DIM - ROT_DIM), jnp.float32)
    zh = jnp.zeros((seq, half), jnp.float32)
    c = jnp.concatenate([cos, cos, ones], axis=1)
    s_up = jnp.concatenate([-sin, zh, zeros], axis=1)
    s_dn = jnp.concatenate([zh, sin, zeros], axis=1)
    rep = LANES // HEAD_DIM
    return jnp.concatenate([jnp.tile(c, (1, rep)), jnp.tile(s_up, (1, rep)), jnp.tile(s_dn, (1, rep))], axis=1)


def _split_hi_lo(w):
    hi = w.astype(MXU_DTYPE)
    lo = (w - hi.astype(jnp.float32)).astype(MXU_DTYPE)
    return jnp.concatenate([hi, lo], axis=-1)


def _trunk(x, p, consts):
    b, seq, d = x.shape
    t = b * seq
    n_assign = t * TOP_K
    n_blocks = -(-n_assign // MOE_BLOCK) + N_EXPERTS
    n_slots = n_blocks * MOE_BLOCK
    tok = jnp.broadcast_to(jnp.arange(t, dtype=jnp.int32)[None, :], (TOP_K, t))

    def layer(x2d, l):
        lidx = l.reshape(1)
        views = _in_proj(x2d, lidx, p["ln_mix"], p["w_in"], p["head_gains"], consts["blockdiag"], consts["rope"], seq)
        merged = None
        for n, dil in enumerate(DILATIONS):
            merged = _band_attn(views[n], dil, b, seq, merged, n == len(DILATIONS) - 1)
        y_d = merged[0]
        bias_tab = _na_bias_table(lax.dynamic_index_in_dim(p["rpb_na"], l, 0, keepdims=False))
        y_n = _na_attn(views[0].reshape(b, seq, 3 * d), bias_tab).reshape(t, W_NA)
        x2d, h, eid, rank, gate, counts = _out_proj_route(
            y_d, y_n, x2d, lidx, p["out_norm_dil"], p["out_norm_na"], p["w_out"], p["ln_ffn"], p["w_route"],
            p["b_route"], consts["tri"])
        counts = counts[:, 0].astype(jnp.int32)
        padded = (counts + MOE_BLOCK - 1) // MOE_BLOCK * MOE_BLOCK
        pend = jnp.cumsum(padded)
        pstart = pend - padded
        experts = jnp.arange(N_EXPERTS, dtype=jnp.int32)[:, None, None]
        dest = jnp.sum(jnp.where(eid[None] == experts, pstart[:, None, None], 0), axis=0) + rank
        block_e = jnp.minimum(jnp.sum(jnp.arange(n_blocks)[:, None] * MOE_BLOCK >= pend[None, :], axis=-1),
                              N_EXPERTS - 1).astype(jnp.int32)
        n_used = (pend[-1:] // MOE_BLOCK).astype(jnp.int32)
        slot_tok = (jnp.arange(n_slots, dtype=jnp.int32) % t).at[dest.reshape(-1)].set(
            tok.reshape(-1), mode="promise_in_bounds", unique_indices=True)
        xs = h.at[slot_tok].get(mode="promise_in_bounds")
        yb = _experts(xs, lidx, block_e, n_used, p["w_gate"], p["w_up"], p["w_down"])
        y_a = yb.at[dest[0]].get(mode="promise_in_bounds")
        y_b = yb.at[dest[1]].get(mode="promise_in_bounds")
        return _combine(y_a, y_b, gate.T, x2d, consts["tri"].shape[0]), None

    x2d, _ = lax.scan(layer, x.reshape(t, d), jnp.arange(DEPTH, dtype=jnp.int32))
    return x2d.reshape(b, seq, d)


def kernel(x_prompt, x_sample, ln_mix, w_in, q_norm_dil, k_norm_dil, q_norm_na, k_norm_na, rpb_na, out_norm_dil,
           out_norm_na, w_out, ln_ffn, w_router_group, b_router_group, w_router_expert, b_router_expert, w_gate,
           w_up, w_down):
    assert x_prompt.shape[1:] == x_sample.shape[1:]
    seq = x_prompt.shape[1]
    assert seq % (max(DILATIONS) * BAND_SIDE) == 0 and seq % (NA_ROWS_STEP * GRID_W) == 0
    depth = ln_mix.shape[0]
    heads_per_group = W_DIL // HEAD_DIM
    head_gains = jnp.stack([jnp.tile(g, (1, heads_per_group)) for g in (q_norm_dil, k_norm_dil, q_norm_na, k_norm_na)],
                           axis=1)
    w_exp = w_router_expert.transpose(0, 2, 1, 3).reshape(depth, D_MODEL, N_EXPERTS)
    w_route = jnp.concatenate([w_exp, w_router_group], axis=-1)
    w_route = jnp.pad(w_route, ((0, 0), (0, 0), (0, LANES - w_route.shape[-1])))
    b_route = jnp.concatenate([b_router_expert.reshape(depth, N_EXPERTS), b_router_group], axis=-1)
    b_route = jnp.pad(b_route, ((0, 0), (0, LANES - b_route.shape[-1])))[:, None, :]
    params = {
        "ln_mix": ln_mix[:, None, :],
        "w_in": w_in.astype(MXU_DTYPE),
        "head_gains": head_gains,
        "rpb_na": rpb_na,
        "out_norm_dil": out_norm_dil[:, None, :],
        "out_norm_na": out_norm_na[:, None, :],
        "w_out": w_out.astype(MXU_DTYPE),
        "ln_ffn": ln_ffn[:, None, :],
        "w_route": _split_hi_lo(w_route),
        "b_route": b_route,
        "w_gate": w_gate,
        "w_up": w_up,
        "w_down": w_down,
    }
    head_of = np.arange(MXU_TILE) // HEAD_DIM
    tm_route = min(TM_ROUTE, (x_prompt.shape[0] + x_sample.shape[0]) * seq)
    consts = {
        "blockdiag": jnp.asarray((head_of[:, None] == head_of[None, :]) / HEAD_DIM, MXU_DTYPE),
        "rope": _rope_table(seq),
        "tri": jnp.asarray(np.arange(tm_route)[:, None] < np.arange(tm_route)[None, :], MXU_DTYPE),
    }
    n_prompt = x_prompt.shape[0]
    y = _trunk(jnp.concatenate([x_prompt, x_sample], axis=0), params, consts)
    return y[:n_prompt], y[n_prompt:]
```

```python
import functools

import numpy as np
import jax
import jax.numpy as jnp
from jax import lax
from jax.experimental import pallas as pl
from jax.experimental.pallas import tpu as pltpu

D_MODEL = 1024
DEPTH = 4
HEAD_DIM = 64
N_HEADS_DIL = 8
N_HEADS_NA = 8
W_DIL = N_HEADS_DIL * HEAD_DIM
W_NA = N_HEADS_NA * HEAD_DIM
QKV_DIL = 3 * W_DIL
DILATIONS = (1, 4, 16)
BAND_SIDE = 64
ROT_DIM = HEAD_DIM // 4
ROPE_THETA = 500000.0
GRID_W = 64
NB_ROWS = 8
NB_COLS = 16
N_GROUPS = 4
EXPERTS_PER_GROUP = 8
N_EXPERTS = N_GROUPS * EXPERTS_PER_GROUP
TOP_K = 2
D_EXPERT = D_MODEL // 2
EPS = 1e-6
NEG = -1e30

LANES = 128
MXU_TILE = 256
LOG2E = 1.4426950408889634
MXU_DTYPE = jnp.bfloat16
VMEM_LIMIT = 48 * 1024 * 1024

TM_PROJ = 512
PROJ_AHEAD = 0
BAND_TILE = {1: 512, 4: 256, 16: 128}
SUB_BAND = 128
BAND_SKEW = {1: (2, 1), 4: (3, 1), 16: (3, 1)}
BAND_ORDER = (16, 4, 1)
NA_ROWS_STEP = 8
NA_ROWS_ITER = 2
NA_SKEW = (3, 1)
TM_ROUTE = 512
MOE_BLOCK = 512
EXPERT_ROW_GROUPS = 2


def _cparams(*sem):
    return pltpu.CompilerParams(dimension_semantics=sem, vmem_limit_bytes=VMEM_LIMIT)


def _in_proj_kernel(l_ref, x_ref, g_ref, w_ref, hg_ref, bd_ref, rope_ref, o_ref, *rest):
    del l_ref
    class_refs, stage_ref = rest[:-1], rest[-1]
    tm = x_ref.shape[0]
    x = x_ref[...]
    ms = jnp.mean(x * x, axis=-1, keepdims=True)
    h = (x * lax.rsqrt(ms + EPS) * g_ref[...]).astype(MXU_DTYPE)
    cos = rope_ref[:, 0:LANES]
    sin_up = rope_ref[:, LANES:2 * LANES]
    sin_dn = rope_ref[:, 2 * LANES:3 * LANES]
    half = ROT_DIM // 2

    def project(c):
        return jnp.dot(h, w_ref[:, c * W_DIL:(c + 1) * W_DIL], preferred_element_type=jnp.float32)

    def finish(c, acc):
        if c in (0, 1, 3, 4):
            gi = (0, 1, None, 2, 3)[c]
            sq = (acc * acc).astype(MXU_DTYPE)
            bw = bd_ref.shape[0]
            msh = jnp.concatenate(
                [jnp.dot(sq[:, j * bw:(j + 1) * bw], bd_ref[...], preferred_element_type=jnp.float32)
                 for j in range(W_DIL // bw)], axis=1)
            acc = acc * lax.rsqrt(msh + EPS) * hg_ref[gi:gi + 1, :]
        if c in (0, 1):
            parts = []
            for j in range(W_DIL // LANES):
                t = acc[:, j * LANES:(j + 1) * LANES]
                parts.append(t * cos + pltpu.roll(t, LANES - half, 1) * sin_up + pltpu.roll(t, half, 1) * sin_dn)
            acc = jnp.concatenate(parts, axis=1)
        if c in (0, 3):
            acc = acc * (HEAD_DIM ** -0.5 * LOG2E)
        o_ref[:, c * W_DIL:(c + 1) * W_DIL] = acc.astype(o_ref.dtype)
        if c < 3:
            for j in range(W_DIL // LANES):
                stage_ref[j] = acc[:, j * LANES:(j + 1) * LANES]
            for dil, cls_ref in zip(DILATIONS[1:], class_refs):
                for r in range(dil):
                    for j in range(W_DIL // LANES):
                        col = r * QKV_DIL + c * W_DIL + j * LANES
                        rows = stage_ref[j, pl.ds(r, tm // dil, stride=dil), :]
                        cls_ref[:, col:col + LANES] = rows.astype(cls_ref.dtype)

    n_chunks = w_ref.shape[1] // W_DIL
    pending = [project(c) for c in range(PROJ_AHEAD)]
    for c in range(n_chunks):
        if c + PROJ_AHEAD < n_chunks:
            pending.append(project(c + PROJ_AHEAD))
        finish(c, pending.pop(0))


def _in_proj(x2d, lidx, ln_g, w_in, head_gains, blockdiag, rope_tab, seq):
    t, d = x2d.shape
    tm = min(TM_PROJ, seq)
    n_seq_blocks = seq // tm
    class_dils = DILATIONS[1:]
    grid_spec = pltpu.PrefetchScalarGridSpec(
        num_scalar_prefetch=1,
        grid=(t // tm,),
        in_specs=[
            pl.BlockSpec((tm, d), lambda i, l: (i, 0)),
            pl.BlockSpec((None, 1, d), lambda i, l: (l[0], 0, 0)),
            pl.BlockSpec((None, d, 3 * d), lambda i, l: (l[0], 0, 0)),
            pl.BlockSpec((None, 4, W_DIL), lambda i, l: (l[0], 0, 0)),
            pl.BlockSpec((MXU_TILE, MXU_TILE), lambda i, l: (0, 0)),
            pl.BlockSpec((tm, 3 * LANES), lambda i, l: (i % n_seq_blocks, 0)),
        ],
        out_specs=[pl.BlockSpec((tm, 3 * d), lambda i, l: (i, 0))]
        + [pl.BlockSpec((tm // dil, dil * QKV_DIL), lambda i, l: (i, 0)) for dil in class_dils],
        scratch_shapes=[pltpu.VMEM((W_DIL // LANES, tm, LANES), jnp.float32)],
    )
    return pl.pallas_call(
        _in_proj_kernel,
        grid_spec=grid_spec,
        out_shape=[jax.ShapeDtypeStruct((t, 3 * d), MXU_DTYPE)]
        + [jax.ShapeDtypeStruct((t // dil, dil * QKV_DIL), MXU_DTYPE) for dil in class_dils],
        compiler_params=_cparams("parallel"),
        name="in_proj",
    )(lidx, x2d, ln_g, w_in, head_gains, blockdiag, rope_tab)


def _pair_rows(x, low_mask):
    zero = jnp.zeros_like(x)
    return jnp.concatenate([jnp.where(low_mask, x, zero), jnp.where(low_mask, zero, x)], axis=0)


def _softmax_pv(s, v):
    m = jnp.max(s, axis=-1, keepdims=True)
    p = jnp.exp2((s - m).astype(MXU_DTYPE))
    v_ones = jnp.concatenate([v, jnp.ones_like(v)], axis=1)
    res = jnp.dot(p, v_ones, preferred_element_type=jnp.float32)
    return res[:, :v.shape[1]], res[:, v.shape[1]:], m


def _staged(work, scores, attend, emit, ahead, lag):
    s_queue, a_queue = [], []
    for step in range(len(work) + ahead + lag):
        if step < len(work):
            s_queue.append(scores(*work[step]))
        if 0 <= step - ahead < len(work):
            a_queue.append(attend(s_queue.pop(0), *work[step - ahead]))
        if 0 <= step - ahead - lag < len(work):
            emit(a_queue.pop(0), *work[step - ahead - lag])


def _band_kernel(*refs, dil, tq, sub, n_l, has_prev, is_last):
    cur_ref, before_ref, after_ref, band_ref = refs[:4]
    pos = 4
    if has_prev:
        po_ref, pl_ref = refs[pos:pos + 2]
        pos += 2
    o_ref = refs[pos]
    pos += 1
    if not is_last:
        lse_ref = refs[pos]

    li = pl.program_id(1)
    side = BAND_SIDE
    nk = sub + 2 * side
    low = lax.broadcasted_iota(jnp.int32, (1, LANES), 1) < HEAD_DIM
    col = lax.broadcasted_iota(jnp.int32, (1, nk), 1)
    n_sub = tq // sub

    def window(j, c0):
        cs = slice(c0, c0 + LANES)
        lo, hi = j * sub - side, (j + 1) * sub + side
        parts = [before_ref[:, cs]] if lo < 0 else []
        parts.append(cur_ref[max(lo, 0):min(hi, tq), cs])
        if hi > tq:
            parts.append(after_ref[:, cs])
        return jnp.concatenate(parts, axis=0)

    def scores(r, hp, j):
        q0 = r * QKV_DIL + hp * LANES
        q2 = _pair_rows(cur_ref[j * sub:(j + 1) * sub, q0:q0 + LANES], low)
        s = lax.dot_general(q2, window(j, q0 + W_DIL), (((1,), (1,)), ((), ())), preferred_element_type=jnp.float32)
        s = s + band_ref[...]
        if j == 0:
            s = jnp.where((col >= side) | (li > 0), s, NEG)
        if j == n_sub - 1:
            s = jnp.where((col < nk - side) | (li < n_l - 1), s, NEG)
        return s

    def attend(s, r, hp, j):
        return _softmax_pv(s, window(j, r * QKV_DIL + 2 * W_DIL + hp * LANES))

    def emit(res, r, hp, j):
        o2, den, m = res
        o2 = o2 * (1.0 / den)
        lse2 = m + jnp.log2(den)
        o = jnp.where(low, o2[:sub], o2[sub:])
        lse = jnp.where(low, lse2[:sub], lse2[sub:])
        if dil == 1:
            rows = slice(j * sub, (j + 1) * sub)
        else:
            rows = pl.ds(dil * j * sub + r, sub, stride=dil)
        if has_prev:
            o_a = po_ref[hp, rows, :]
            lse_a = pl_ref[hp, rows, :]
            mx = jnp.maximum(lse_a, lse)
            w_a = jnp.exp2(lse_a - mx)
            w_b = jnp.exp2(lse - mx)
            tot = w_a + w_b
            o = (w_a * o_a + w_b * o) * (1.0 / tot)
            lse = mx + jnp.log2(tot)
        o_ref[hp, rows, :] = o
        if not is_last:
            lse_ref[hp, rows, :] = lse

    work = [(r, hp, j) for r in range(dil) for hp in range(W_DIL // LANES) for j in range(n_sub)]
    ahead, lag = BAND_SKEW[dil]
    _staged(work, scores, attend, emit, ahead, lag)


def _band_attn(cls, dil, batch, seq, prev, is_last):
    l_len = seq // dil
    tq = min(BAND_TILE[dil], l_len)
    sub = min(SUB_BAND, tq)
    n_l = l_len // tq
    halo_per_tile = tq // BAND_SIDE
    n_halo = l_len // BAND_SIDE
    width = dil * QKV_DIL

    cur = pl.BlockSpec((tq, width), lambda bi, l: (bi * n_l + l, 0))
    before = pl.BlockSpec((BAND_SIDE, width),
                          lambda bi, l: (bi * n_halo + jnp.maximum(l * halo_per_tile - 1, 0), 0))
    after = pl.BlockSpec((BAND_SIDE, width),
                         lambda bi, l: (bi * n_halo + jnp.minimum((l + 1) * halo_per_tile, n_halo - 1), 0))
    n_pairs = W_DIL // LANES
    nat = pl.BlockSpec((n_pairs, dil * tq, LANES), lambda bi, l: (0, bi * n_l + l, 0))
    band = _band_bias(sub)
    in_specs = [cur, before, after, pl.BlockSpec(band.shape, lambda bi, l: (0, 0))]
    args = [cls, cls, cls, band]
    has_prev = prev is not None
    if has_prev:
        in_specs += [nat, nat]
        args += list(prev)
    o_shape = jax.ShapeDtypeStruct((n_pairs, batch * seq, LANES), jnp.float32)
    n_out = 1 if is_last else 2
    return pl.pallas_call(
        functools.partial(_band_kernel, dil=dil, tq=tq, sub=sub, n_l=n_l, has_prev=has_prev, is_last=is_last),
        grid=(batch, n_l),
        in_specs=in_specs,
        out_specs=[nat] * n_out,
        out_shape=[o_shape] * n_out,
        compiler_params=_cparams("parallel", "parallel"),
        name=f"band_attn_d{dil}",
    )(*args)


def _band_bias(sub):
    nk = sub + 2 * BAND_SIDE
    r = np.arange(2 * sub)[:, None] % sub
    c = np.arange(nk)[None, :]
    ok = (c - r >= 0) & (c - r <= 2 * BAND_SIDE)
    return jnp.asarray(np.where(ok, 0.0, NEG), jnp.float32)


def _na_kernel(q_ref, kc_ref, kp_ref, kn_ref, vc_ref, vp_ref, vn_ref, bias_ref, o_ref, kk_ref, vv_ref, *, n_rows):
    step = pl.program_id(1)
    halo = (NB_ROWS // 2) * GRID_W
    cur = NA_ROWS_STEP * GRID_W
    kk_ref[0:halo, :] = kp_ref[...]
    kk_ref[halo:halo + cur, :] = kc_ref[...]
    kk_ref[halo + cur:, :] = kn_ref[...]
    vv_ref[0:halo, :] = vp_ref[...]
    vv_ref[halo:halo + cur, :] = vc_ref[...]
    vv_ref[halo + cur:, :] = vn_ref[...]

    low = lax.broadcasted_iota(jnp.int32, (1, LANES), 1) < HEAD_DIM
    n_keys = NB_ROWS * GRID_W
    row_base = step * NA_ROWS_STEP

    def rows_iter(it, carry):
        work = []
        for u in range(NA_ROWS_ITER):
            i = it * NA_ROWS_ITER + u
            r = row_base + i
            r0 = jnp.clip(r - NB_ROWS // 2, 0, n_rows - NB_ROWS)
            start = pl.multiple_of((r0 - row_base + NB_ROWS // 2) * GRID_W, GRID_W)
            qrow = pl.multiple_of(i * GRID_W, GRID_W)
            work += [(hp, qrow, start, r - r0) for hp in range(W_NA // LANES)]

        def scores(hp, qrow, start, variant):
            cs = slice(hp * LANES, (hp + 1) * LANES)
            q2 = _pair_rows(q_ref[pl.ds(qrow, GRID_W), cs], low)
            keys = kk_ref[pl.ds(start, n_keys), cs]
            s = lax.dot_general(q2, keys, (((1,), (1,)), ((), ())), preferred_element_type=jnp.float32)
            return s + bias_ref[hp, variant]

        def attend(s, hp, qrow, start, variant):
            return _softmax_pv(s, vv_ref[pl.ds(start, n_keys), hp * LANES:(hp + 1) * LANES])

        def emit(res, hp, qrow, start, variant):
            o2, den, _ = res
            o2 = o2 * (1.0 / den)
            o_ref[pl.ds(qrow, GRID_W), hp * LANES:(hp + 1) * LANES] = jnp.where(low, o2[:GRID_W], o2[GRID_W:])

        _staged(work, scores, attend, emit, *NA_SKEW)
        return carry

    lax.fori_loop(0, NA_ROWS_STEP // NA_ROWS_ITER, rows_iter, 0)


def _na_attn(proj, bias_tab):
    b, seq, width = proj.shape
    n_rows = seq // GRID_W
    cur = NA_ROWS_STEP * GRID_W
    halo = (NB_ROWS // 2) * GRID_W
    halo_per_step = cur // halo
    n_halo = seq // halo
    n_pairs = W_NA // LANES

    def cur_spec(which):
        return pl.BlockSpec((None, cur, W_NA), lambda bi, i: (bi, i, which))

    def before(which):
        return pl.BlockSpec((None, halo, W_NA), lambda bi, i: (bi, jnp.maximum(i * halo_per_step - 1, 0), which))

    def after(which):
        return pl.BlockSpec((None, halo, W_NA),
                            lambda bi, i: (bi, jnp.minimum((i + 1) * halo_per_step, n_halo - 1), which))

    return pl.pallas_call(
        functools.partial(_na_kernel, n_rows=n_rows),
        grid=(b, n_rows // NA_ROWS_STEP),
        in_specs=[cur_spec(3), cur_spec(4), before(4), after(4), cur_spec(5), before(5), after(5),
                  pl.BlockSpec((n_pairs, NB_ROWS, 2 * GRID_W, NB_ROWS * GRID_W), lambda bi, i: (0, 0, 0, 0))],
        out_specs=pl.BlockSpec((None, cur, W_NA), lambda bi, i: (bi, i, 0)),
        out_shape=jax.ShapeDtypeStruct((b, seq, W_NA), jnp.float32),
        scratch_shapes=[pltpu.VMEM((cur + 2 * halo, W_NA), MXU_DTYPE),
                        pltpu.VMEM((cur + 2 * halo, W_NA), MXU_DTYPE)],
        compiler_params=_cparams("parallel", "parallel"),
        name="na_attn",
    )(proj, proj, proj, proj, proj, proj, proj, bias_tab)


def _na_bias_table(rpb):
    c = np.arange(GRID_W)
    wstart = np.clip(c - NB_COLS // 2, 0, GRID_W - NB_COLS)
    kc = np.arange(GRID_W)
    valid = (kc[None, :] >= wstart[:, None]) & (kc[None, :] < wstart[:, None] + NB_COLS)
    dc = np.clip(kc[None, :] - c[:, None] + NB_COLS - 1, 0, 2 * NB_COLS - 2)
    pick = jnp.asarray(dc[None, :, :] == np.arange(2 * NB_COLS - 1)[:, None, None])
    by_col = jnp.sum(jnp.where(pick[None, None], rpb.astype(jnp.float32)[:, :, :, None, None], 0.0), axis=2)
    by_col = jnp.where(jnp.asarray(valid)[None, None], by_col * LOG2E, NEG)
    tab = jnp.stack([by_col[:, NB_ROWS - 1 - v:2 * NB_ROWS - 1 - v] for v in range(NB_ROWS)], axis=1)
    tab = tab.transpose(0, 1, 3, 2, 4)
    n_pairs = N_HEADS_NA // 2
    tab = tab.reshape(n_pairs, 2, NB_ROWS, GRID_W, NB_ROWS * GRID_W).transpose(0, 2, 1, 3, 4)
    return tab.reshape(n_pairs, NB_ROWS, 2 * GRID_W, NB_ROWS * GRID_W)


def _out_route_kernel(l_ref, yd_ref, yn_ref, x_ref, gd_ref, gn_ref, w_ref, g_ref, wr_ref, b_ref, tri_ref,
                      o_ref, h_ref, eid_ref, rank_ref, gate_ref, cnt_ref, carry_ref):
    del l_ref

    def norm(y, g):
        ms = jnp.mean(y * y, axis=-1, keepdims=True)
        return (y * lax.rsqrt(ms + EPS) * g).astype(MXU_DTYPE)

    nd = norm(jnp.concatenate([yd_ref[j] for j in range(yd_ref.shape[0])], axis=1), gd_ref[...])
    nn = norm(yn_ref[...], gn_ref[...])
    y = jnp.dot(nd, w_ref[0:W_DIL, :], preferred_element_type=jnp.float32)
    y = y + jnp.dot(nn, w_ref[W_DIL:, :], preferred_element_type=jnp.float32)
    x = x_ref[...] + y
    o_ref[...] = x
    _route_tile(x, g_ref, wr_ref, b_ref, tri_ref, h_ref, eid_ref, rank_ref, gate_ref, cnt_ref, carry_ref)


def _route_tile(x, g_ref, w_ref, b_ref, tri_ref, h_ref, eid_ref, rank_ref, gate_ref, cnt_ref, carry_ref):
    @pl.when(pl.program_id(0) == 0)
    def _():
        carry_ref[...] = jnp.zeros_like(carry_ref)

    ms = jnp.mean(x * x, axis=-1, keepdims=True)
    h = x * lax.rsqrt(ms + EPS) * g_ref[...]
    h_hi = h.astype(MXU_DTYPE)
    h_ref[...] = h_hi
    h_lo = (h - h_hi.astype(jnp.float32)).astype(MXU_DTYPE)
    both = jnp.dot(h_hi, w_ref[...], preferred_element_type=jnp.float32)
    logits = both[:, :LANES] + both[:, LANES:]
    logits = logits + jnp.dot(h_lo, w_ref[:, :LANES], preferred_element_type=jnp.float32) + b_ref[...]
    lt = logits.T
    tm = lt.shape[1]

    best = lt[N_EXPERTS:N_EXPERTS + 1]
    gsel = jnp.zeros_like(best)
    for g in range(1, N_GROUPS):
        cand = lt[N_EXPERTS + g:N_EXPERTS + g + 1]
        upd = cand > best
        gsel = jnp.where(upd, float(g), gsel)
        best = jnp.where(upd, cand, best)
    den = jnp.zeros_like(best)
    for g in range(N_GROUPS):
        den = den + jnp.exp(lt[N_EXPERTS + g:N_EXPERTS + g + 1] - best)
    g_gate = 1.0 / den

    e8 = EXPERTS_PER_GROUP
    sel = lt[0:e8]
    for g in range(1, N_GROUPS):
        sel = jnp.where(gsel == float(g), lt[g * e8:(g + 1) * e8], sel)
    row = lax.broadcasted_iota(jnp.int32, (e8, tm), 0).astype(jnp.float32)
    v1 = jnp.max(sel, axis=0, keepdims=True)
    i1 = jnp.min(jnp.where(sel == v1, row, float(e8)), axis=0, keepdims=True)
    rest = jnp.where(row == i1, -jnp.inf, sel)
    v2 = jnp.max(rest, axis=0, keepdims=True)
    i2 = jnp.min(jnp.where(rest == v2, row, float(e8)), axis=0, keepdims=True)
    e2 = jnp.exp(v2 - v1)
    inv = 1.0 / (1.0 + e2)
    gate_ref[0:1, :] = g_gate * inv
    gate_ref[1:2, :] = g_gate * (e2 * inv)
    eid1 = gsel * float(e8) + i1
    eid2 = gsel * float(e8) + i2
    eid_ref[0:1, :] = eid1.astype(jnp.int32)
    eid_ref[1:2, :] = eid2.astype(jnp.int32)

    erow = lax.broadcasted_iota(jnp.int32, (N_EXPERTS, tm), 0).astype(jnp.float32)
    hit1 = erow == eid1
    hit2 = erow == eid2
    oh = jnp.concatenate([hit1, hit2], axis=0).astype(jnp.float32)
    pref = jnp.dot(oh.astype(MXU_DTYPE), tri_ref[...], preferred_element_type=jnp.float32)
    tot = jnp.sum(oh, axis=1, keepdims=True)
    carry = carry_ref[:, 0:1]
    val1 = carry + pref[:N_EXPERTS]
    val2 = carry + tot[:N_EXPERTS] + pref[N_EXPERTS:]
    rank_ref[0:1, :] = jnp.sum(jnp.where(hit1, val1, 0.0), axis=0, keepdims=True).astype(jnp.int32)
    rank_ref[1:2, :] = jnp.sum(jnp.where(hit2, val2, 0.0), axis=0, keepdims=True).astype(jnp.int32)
    new_carry = carry_ref[...] + (tot[:N_EXPERTS] + tot[N_EXPERTS:])
    carry_ref[...] = new_carry
    cnt_ref[...] = new_carry


def _out_proj_route(y_d, y_n, x2d, lidx, g_d, g_n, w_out, ln_g, w_route, b_route, tri):
    t, d = x2d.shape
    tm = tri.shape[0]
    per_tile = lambda i, l: (i, 0)
    per_layer = lambda i, l: (l[0], 0, 0)
    per_token_cols = lambda i, l: (0, i)
    grid_spec = pltpu.PrefetchScalarGridSpec(
        num_scalar_prefetch=1,
        grid=(t // tm,),
        in_specs=[
            pl.BlockSpec((W_DIL // LANES, tm, LANES), lambda i, l: (0, i, 0)),
            pl.BlockSpec((tm, W_NA), per_tile),
            pl.BlockSpec((tm, d), per_tile),
            pl.BlockSpec((None, 1, W_DIL), per_layer),
            pl.BlockSpec((None, 1, W_NA), per_layer),
            pl.BlockSpec((None, d, d), per_layer),
            pl.BlockSpec((None, 1, d), per_layer),
            pl.BlockSpec((None, d, 2 * LANES), per_layer),
            pl.BlockSpec((None, 1, LANES), per_layer),
            pl.BlockSpec((tm, tm), lambda i, l: (0, 0)),
        ],
        out_specs=[
            pl.BlockSpec((tm, d), per_tile),
            pl.BlockSpec((tm, d), per_tile),
            pl.BlockSpec((TOP_K, tm), per_token_cols),
            pl.BlockSpec((TOP_K, tm), per_token_cols),
            pl.BlockSpec((TOP_K, tm), per_token_cols),
            pl.BlockSpec((N_EXPERTS, LANES), lambda i, l: (0, 0)),
        ],
        scratch_shapes=[pltpu.VMEM((N_EXPERTS, LANES), jnp.float32)],
    )
    return pl.pallas_call(
        _out_route_kernel,
        grid_spec=grid_spec,
        out_shape=[
            jax.ShapeDtypeStruct((t, d), jnp.float32),
            jax.ShapeDtypeStruct((t, d), MXU_DTYPE),
            jax.ShapeDtypeStruct((TOP_K, t), jnp.int32),
            jax.ShapeDtypeStruct((TOP_K, t), jnp.int32),
            jax.ShapeDtypeStruct((TOP_K, t), jnp.float32),
            jax.ShapeDtypeStruct((N_EXPERTS, LANES), jnp.float32),
        ],
        compiler_params=_cparams("arbitrary"),
        name="out_proj_route",
    )(lidx, y_d, y_n, x2d, g_d, g_n, w_out, ln_g, w_route, b_route, tri)


def _expert_kernel(l_ref, be_ref, nu_ref, x_ref, wg_ref, wu_ref, wd_ref, o_ref, cg_ref, cu_ref, cd_ref):
    del l_ref
    i = pl.program_id(0)

    @pl.when((i == 0) | (be_ref[i] != be_ref[jnp.maximum(i - 1, 0)]))
    def _():
        cg_ref[...] = wg_ref[...].astype(MXU_DTYPE)
        cu_ref[...] = wu_ref[...].astype(MXU_DTYPE)
        cd_ref[...] = wd_ref[...].astype(MXU_DTYPE)

    @pl.when(i < nu_ref[0])
    def _():
        rows = x_ref.shape[0] // EXPERT_ROW_GROUPS
        gu = []
        for n in range(EXPERT_ROW_GROUPS):
            x = x_ref[n * rows:(n + 1) * rows, :]
            gu.append((jnp.dot(x, cg_ref[...], preferred_element_type=jnp.float32),
                       jnp.dot(x, cu_ref[...], preferred_element_type=jnp.float32)))
        for n, (g, u) in enumerate(gu):
            a = (g * jax.nn.sigmoid(g) * u).astype(MXU_DTYPE)
            y = jnp.dot(a, cd_ref[...], preferred_element_type=jnp.float32)
            o_ref[n * rows:(n + 1) * rows, :] = y.astype(o_ref.dtype)

    @pl.when(i >= nu_ref[0])
    def _():
        o_ref[...] = jnp.zeros_like(o_ref)


def _experts(xs, lidx, block_e, n_used, w_gate, w_up, w_down):
    n_slots, d = xs.shape
    n_blocks = n_slots // MOE_BLOCK
    grid_spec = pltpu.PrefetchScalarGridSpec(
        num_scalar_prefetch=3,
        grid=(n_blocks,),
        in_specs=[
            pl.BlockSpec((MOE_BLOCK, d), lambda i, l, be, nu: (i, 0)),
            pl.BlockSpec((None, None, d, D_EXPERT), lambda i, l, be, nu: (l[0], be[i], 0, 0)),
            pl.BlockSpec((None, None, d, D_EXPERT), lambda i, l, be, nu: (l[0], be[i], 0, 0)),
            pl.BlockSpec((None, None, D_EXPERT, d), lambda i, l, be, nu: (l[0], be[i], 0, 0)),
        ],
        out_specs=pl.BlockSpec((MOE_BLOCK, d), lambda i, l, be, nu: (i, 0)),
        scratch_shapes=[pltpu.VMEM((d, D_EXPERT), MXU_DTYPE), pltpu.VMEM((d, D_EXPERT), MXU_DTYPE),
                        pltpu.VMEM((D_EXPERT, d), MXU_DTYPE)],
    )
    return pl.pallas_call(
        _expert_kernel,
        grid_spec=grid_spec,
        out_shape=jax.ShapeDtypeStruct((n_slots, d), MXU_DTYPE),
        compiler_params=_cparams("arbitrary"),
        name="experts",
    )(lidx, block_e, n_used, xs, w_gate, w_up, w_down)


def _combine_kernel(ya_ref, yb_ref, gate_ref, x_ref, o_ref):
    ga = gate_ref[:, 0:1]
    gb = gate_ref[:, 1:2]
    o_ref[...] = x_ref[...] + (ya_ref[...].astype(jnp.float32) * ga + yb_ref[...].astype(jnp.float32) * gb)


def _combine(y_a, y_b, gate_t, x2d, tm):
    t, d = x2d.shape
    row = pl.BlockSpec((tm, d), lambda i: (i, 0))
    return pl.pallas_call(
        _combine_kernel,
        grid=(t // tm,),
        in_specs=[row, row, pl.BlockSpec((tm, TOP_K), lambda i: (i, 0)), row],
        out_specs=row,
        out_shape=jax.ShapeDtypeStruct((t, d), jnp.float32),
        compiler_params=_cparams("parallel"),
        name="combine",
    )(y_a, y_b, gate_t, x2d)


def _rope_table(seq):
    half = ROT_DIM // 2
    inv = ROPE_THETA ** (-jnp.arange(half, dtype=jnp.float32) / half)
    ang = jnp.arange(seq).astype(jnp.float32)[:, None] * inv[None, :]
    cos, sin = jnp.cos(ang), jnp.sin(ang)
    ones = jnp.ones((seq, HEAD_DIM - ROT_DIM), jnp.float32)
    zeros = jnp.zeros((seq, HEAD_DIM - ROT_DIM), jnp.float32)
    zh = jnp.zeros((seq, half), jnp.float32)
    c = jnp.concatenate([cos, cos, ones], axis=1)
    s_up = jnp.concatenate([-sin, zh, zeros], axis=1)
    s_dn = jnp.concatenate([zh, sin, zeros], axis=1)
    rep = LANES // HEAD_DIM
    return jnp.concatenate([jnp.tile(c, (1, rep)), jnp.tile(s_up, (1, rep)), jnp.tile(s_dn, (1, rep))], axis=1)


def _split_hi_lo(w):
    hi = w.astype(MXU_DTYPE)
    lo = (w - hi.astype(jnp.float32)).astype(MXU_DTYPE)
    return jnp.concatenate([hi, lo], axis=-1)


def _trunk(x, p, consts):
    b, seq, d = x.shape
    t = b * seq
    n_assign = t * TOP_K
    n_blocks = -(-n_assign // MOE_BLOCK) + N_EXPERTS
    n_slots = n_blocks * MOE_BLOCK
    tok = jnp.broadcast_to(jnp.arange(t, dtype=jnp.int32)[None, :], (TOP_K, t))

    def layer(x2d, l):
        lidx = l.reshape(1)
        views = _in_proj(x2d, lidx, p["ln_mix"], p["w_in"], p["head_gains"], consts["blockdiag"], consts["rope"], seq)
        merged = None
        for n, dil in enumerate(BAND_ORDER):
            merged = _band_attn(views[DILATIONS.index(dil)], dil, b, seq, merged, n == len(BAND_ORDER) - 1)
        y_d = merged[0]
        bias_tab = _na_bias_table(lax.dynamic_index_in_dim(p["rpb_na"], l, 0, keepdims=False))
        y_n = _na_attn(views[0].reshape(b, seq, 3 * d), bias_tab).reshape(t, W_NA)
        x2d, h, eid, rank, gate, counts = _out_proj_route(
            y_d, y_n, x2d, lidx, p["out_norm_dil"], p["out_norm_na"], p["w_out"], p["ln_ffn"], p["w_route"],
            p["b_route"], consts["tri"])
        counts = counts[:, 0].astype(jnp.int32)
        padded = (counts + MOE_BLOCK - 1) // MOE_BLOCK * MOE_BLOCK
        pend = jnp.cumsum(padded)
        pstart = pend - padded
        experts = jnp.arange(N_EXPERTS, dtype=jnp.int32)[:, None, None]
        dest = jnp.sum(jnp.where(eid[None] == experts, pstart[:, None, None], 0), axis=0) + rank
        block_e = jnp.minimum(jnp.sum(jnp.arange(n_blocks)[:, None] * MOE_BLOCK >= pend[None, :], axis=-1),
                              N_EXPERTS - 1).astype(jnp.int32)
        n_used = (pend[-1:] // MOE_BLOCK).astype(jnp.int32)
        slot_tok = (jnp.arange(n_slots, dtype=jnp.int32) % t).at[dest.reshape(-1)].set(
            tok.reshape(-1), mode="promise_in_bounds", unique_indices=True)
        xs = h.at[slot_tok].get(mode="promise_in_bounds")
        yb = _experts(xs, lidx, block_e, n_used, p["w_gate"], p["w_up"], p["w_down"])
        y_a = yb.at[dest[0]].get(mode="promise_in_bounds")
        y_b = yb.at[dest[1]].get(mode="promise_in_bounds")
        return _combine(y_a, y_b, gate.T, x2d, consts["tri"].shape[0]), None

    x2d, _ = lax.scan(layer, x.reshape(t, d), jnp.arange(DEPTH, dtype=jnp.int32))
    return x2d.reshape(b, seq, d)


def kernel(x_prompt, x_sample, ln_mix, w_in, q_norm_dil, k_norm_dil, q_norm_na, k_norm_na, rpb_na, out_norm_dil,
           out_norm_na, w_out, ln_ffn, w_router_group, b_router_group, w_router_expert, b_router_expert, w_gate,
           w_up, w_down):
    assert x_prompt.shape[1:] == x_sample.shape[1:]
    seq = x_prompt.shape[1]
    assert seq % (max(DILATIONS) * BAND_SIDE) == 0 and seq % (NA_ROWS_STEP * GRID_W) == 0
    depth = ln_mix.shape[0]
    heads_per_group = W_DIL // HEAD_DIM
    head_gains = jnp.stack([jnp.tile(g, (1, heads_per_group)) for g in (q_norm_dil, k_norm_dil, q_norm_na, k_norm_na)],
                           axis=1)
    w_exp = w_router_expert.transpose(0, 2, 1, 3).reshape(depth, D_MODEL, N_EXPERTS)
    w_route = jnp.concatenate([w_exp, w_router_group], axis=-1)
    w_route = jnp.pad(w_route, ((0, 0), (0, 0), (0, LANES - w_route.shape[-1])))
    b_route = jnp.concatenate([b_router_expert.reshape(depth, N_EXPERTS), b_router_group], axis=-1)
    b_route = jnp.pad(b_route, ((0, 0), (0, LANES - b_route.shape[-1])))[:, None, :]
    params = {
        "ln_mix": ln_mix[:, None, :],
        "w_in": w_in.astype(MXU_DTYPE),
        "head_gains": head_gains,
        "rpb_na": rpb_na,
        "out_norm_dil": out_norm_dil[:, None, :],
        "out_norm_na": out_norm_na[:, None, :],
        "w_out": w_out.astype(MXU_DTYPE),
        "ln_ffn": ln_ffn[:, None, :],
        "w_route": _split_hi_lo(w_route),
        "b_route": b_route,
        "w_gate": w_gate,
        "w_up": w_up,
        "w_down": w_down,
    }
    head_of = np.arange(MXU_TILE) // HEAD_DIM
    tm_route = min(TM_ROUTE, (x_prompt.shape[0] + x_sample.shape[0]) * seq)
    consts = {
        "blockdiag": jnp.asarray((head_of[:, None] == head_of[None, :]) / HEAD_DIM, MXU_DTYPE),
        "rope": _rope_table(seq),
        "tri": jnp.asarray(np.arange(tm_route)[:, None] < np.arange(tm_route)[None, :], MXU_DTYPE),
    }
    n_prompt = x_prompt.shape[0]
    y = _trunk(jnp.concatenate([x_prompt, x_sample], axis=0), params, consts)
    return y[:n_prompt], y[n_prompt:]
```

```python
import functools

import numpy as np
import jax
import jax.numpy as jnp
from jax import lax
from jax.experimental import pallas as pl
from jax.experimental.pallas import tpu as pltpu

D_MODEL = 1024
DEPTH = 4
HEAD_DIM = 64
N_HEADS_DIL = 8
N_HEADS_NA = 8
W_DIL = N_HEADS_DIL * HEAD_DIM
W_NA = N_HEADS_NA * HEAD_DIM
QKV_DIL = 3 * W_DIL
DILATIONS = (1, 4, 16)
BAND_SIDE = 64
ROT_DIM = HEAD_DIM // 4
ROPE_THETA = 500000.0
GRID_W = 64
NB_ROWS = 8
NB_COLS = 16
N_GROUPS = 4
EXPERTS_PER_GROUP = 8
N_EXPERTS = N_GROUPS * EXPERTS_PER_GROUP
TOP_K = 2
D_EXPERT = D_MODEL // 2
EPS = 1e-6
NEG = -1e30

LANES = 128
MXU_TILE = 256
LOG2E = 1.4426950408889634
MXU_DTYPE = jnp.bfloat16
VMEM_LIMIT = 48 * 1024 * 1024

TM_PROJ = 512
BAND_TILE = {1: 512, 4: 256, 16: 128}
SUB_BAND = 128
BAND_SKEW = {1: (2, 1), 4: (3, 1), 16: (3, 1)}
BAND_ORDER = (16, 4, 1)
NA_ROWS_STEP = 16
NA_SKEW = (3, 1)
TM_ROUTE = 512
MOE_BLOCK = 512
EXPERT_ROW_GROUPS = 2


def _cparams(*sem):
    return pltpu.CompilerParams(dimension_semantics=sem, vmem_limit_bytes=VMEM_LIMIT)


def _pair_specs(tm, d, n_first):
    first = pl.BlockSpec((tm, d), lambda i, *_: (jnp.minimum(i, n_first - 1), 0))
    second = pl.BlockSpec((tm, d), lambda i, *_: (jnp.maximum(i - n_first, 0), 0))
    return [first, second]


def _on_part(n_first, body, first_refs, second_refs):
    @pl.when(pl.program_id(0) < n_first)
    def _():
        body(*first_refs)

    @pl.when(pl.program_id(0) >= n_first)
    def _():
        body(*second_refs)


def _in_proj_kernel(l_ref, xa_ref, xb_ref, *rest, n_first):
    del l_ref
    _on_part(n_first, lambda x_ref: _in_proj_tile(x_ref, *rest), (xa_ref,), (xb_ref,))


def _in_proj_tile(x_ref, g_ref, w_ref, hg_ref, bd_ref, rope_ref, o_ref, *rest):
    class_refs, stage_ref = rest[:-1], rest[-1]
    tm = x_ref.shape[0]
    x = x_ref[...]
    ms = jnp.mean(x * x, axis=-1, keepdims=True)
    h = (x * lax.rsqrt(ms + EPS) * g_ref[...]).astype(MXU_DTYPE)
    cos = rope_ref[:, 0:LANES]
    sin_up = rope_ref[:, LANES:2 * LANES]
    sin_dn = rope_ref[:, 2 * LANES:3 * LANES]
    half = ROT_DIM // 2

    def project(c):
        return jnp.dot(h, w_ref[:, c * W_DIL:(c + 1) * W_DIL], preferred_element_type=jnp.float32)

    def finish(c, acc):
        if c in (0, 1, 3, 4):
            gi = (0, 1, None, 2, 3)[c]
            sq = (acc * acc).astype(MXU_DTYPE)
            bw = bd_ref.shape[0]
            msh = jnp.concatenate(
                [jnp.dot(sq[:, j * bw:(j + 1) * bw], bd_ref[...], preferred_element_type=jnp.float32)
                 for j in range(W_DIL // bw)], axis=1)
            acc = acc * lax.rsqrt(msh + EPS) * hg_ref[gi:gi + 1, :]
        if c in (0, 1):
            parts = []
            for j in range(W_DIL // LANES):
                t = acc[:, j * LANES:(j + 1) * LANES]
                parts.append(t * cos + pltpu.roll(t, LANES - half, 1) * sin_up + pltpu.roll(t, half, 1) * sin_dn)
            acc = jnp.concatenate(parts, axis=1)
        if c in (0, 3):
            acc = acc * (HEAD_DIM ** -0.5 * LOG2E)
        o_ref[:, c * W_DIL:(c + 1) * W_DIL] = acc.astype(o_ref.dtype)
        if c < 3:
            for j in range(W_DIL // LANES):
                stage_ref[j] = acc[:, j * LANES:(j + 1) * LANES]
            for dil, cls_ref in zip(DILATIONS[1:], class_refs):
                for r in range(dil):
                    for j in range(W_DIL // LANES):
                        col = r * QKV_DIL + c * W_DIL + j * LANES
                        rows = stage_ref[j, pl.ds(r, tm // dil, stride=dil), :]
                        cls_ref[:, col:col + LANES] = rows.astype(cls_ref.dtype)

    for c in range(w_ref.shape[1] // W_DIL):
        finish(c, project(c))


def _in_proj(xa, xb, lidx, ln_g, w_in, head_gains, blockdiag, rope_tab, seq):
    d = xa.shape[1]
    t = xa.shape[0] + xb.shape[0]
    tm = min(TM_PROJ, seq)
    n_seq_blocks = seq // tm
    n_first = xa.shape[0] // tm
    class_dils = DILATIONS[1:]
    grid_spec = pltpu.PrefetchScalarGridSpec(
        num_scalar_prefetch=1,
        grid=(t // tm,),
        in_specs=_pair_specs(tm, d, n_first) + [
            pl.BlockSpec((None, 1, d), lambda i, l: (l[0], 0, 0)),
            pl.BlockSpec((None, d, 3 * d), lambda i, l: (l[0], 0, 0)),
            pl.BlockSpec((None, 4, W_DIL), lambda i, l: (l[0], 0, 0)),
            pl.BlockSpec((MXU_TILE, MXU_TILE), lambda i, l: (0, 0)),
            pl.BlockSpec((tm, 3 * LANES), lambda i, l: (i % n_seq_blocks, 0)),
        ],
        out_specs=[pl.BlockSpec((tm, 3 * d), lambda i, l: (i, 0))]
        + [pl.BlockSpec((tm // dil, dil * QKV_DIL), lambda i, l: (i, 0)) for dil in class_dils],
        scratch_shapes=[pltpu.VMEM((W_DIL // LANES, tm, LANES), jnp.float32)],
    )
    return pl.pallas_call(
        functools.partial(_in_proj_kernel, n_first=n_first),
        grid_spec=grid_spec,
        out_shape=[jax.ShapeDtypeStruct((t, 3 * d), MXU_DTYPE)]
        + [jax.ShapeDtypeStruct((t // dil, dil * QKV_DIL), MXU_DTYPE) for dil in class_dils],
        compiler_params=_cparams("parallel"),
        name="in_proj",
    )(lidx, xa, xb, ln_g, w_in, head_gains, blockdiag, rope_tab)


def _pair_rows(x, low_mask):
    zero = jnp.zeros_like(x)
    return jnp.concatenate([jnp.where(low_mask, x, zero), jnp.where(low_mask, zero, x)], axis=0)


def _softmax_pv(s, v):
    m = jnp.max(s, axis=-1, keepdims=True)
    p = jnp.exp2((s - m).astype(MXU_DTYPE))
    v_ones = jnp.concatenate([v, jnp.ones_like(v)], axis=1)
    res = jnp.dot(p, v_ones, preferred_element_type=jnp.float32)
    return res[:, :v.shape[1]], res[:, v.shape[1]:], m


def _staged(work, scores, attend, emit, ahead, lag):
    s_queue, a_queue = [], []
    for step in range(len(work) + ahead + lag):
        if step < len(work):
            s_queue.append(scores(*work[step]))
        if 0 <= step - ahead < len(work):
            a_queue.append(attend(s_queue.pop(0), *work[step - ahead]))
        if 0 <= step - ahead - lag < len(work):
            emit(a_queue.pop(0), *work[step - ahead - lag])


def _band_kernel(*refs, dil, tq, sub, n_l, has_prev, is_last):
    cur_ref, before_ref, after_ref, band_ref = refs[:4]
    pos = 4
    if has_prev:
        po_ref, pl_ref = refs[pos:pos + 2]
        pos += 2
    o_ref = refs[pos]
    pos += 1
    if not is_last:
        lse_ref = refs[pos]

    li = pl.program_id(1)
    side = BAND_SIDE
    nk = sub + 2 * side
    low = lax.broadcasted_iota(jnp.int32, (1, LANES), 1) < HEAD_DIM
    col = lax.broadcasted_iota(jnp.int32, (1, nk), 1)
    n_sub = tq // sub

    def window(j, c0):
        cs = slice(c0, c0 + LANES)
        lo, hi = j * sub - side, (j + 1) * sub + side
        parts = [before_ref[:, cs]] if lo < 0 else []
        parts.append(cur_ref[max(lo, 0):min(hi, tq), cs])
        if hi > tq:
            parts.append(after_ref[:, cs])
        return jnp.concatenate(parts, axis=0)

    def scores(r, hp, j):
        q0 = r * QKV_DIL + hp * LANES
        q2 = _pair_rows(cur_ref[j * sub:(j + 1) * sub, q0:q0 + LANES], low)
        s = lax.dot_general(q2, window(j, q0 + W_DIL), (((1,), (1,)), ((), ())), preferred_element_type=jnp.float32)
        s = s + band_ref[...]
        if j == 0:
            s = jnp.where((col >= side) | (li > 0), s, NEG)
        if j == n_sub - 1:
            s = jnp.where((col < nk - side) | (li < n_l - 1), s, NEG)
        return s

    def attend(s, r, hp, j):
        return _softmax_pv(s, window(j, r * QKV_DIL + 2 * W_DIL + hp * LANES))

    def emit(res, r, hp, j):
        o2, den, m = res
        o2 = o2 * (1.0 / den)
        lse2 = m + jnp.log2(den)
        o = jnp.where(low, o2[:sub], o2[sub:])
        lse = jnp.where(low, lse2[:sub], lse2[sub:])
        if dil == 1:
            rows = slice(j * sub, (j + 1) * sub)
        else:
            rows = pl.ds(dil * j * sub + r, sub, stride=dil)
        if has_prev:
            o_a = po_ref[hp, rows, :]
            lse_a = pl_ref[hp, rows, :]
            mx = jnp.maximum(lse_a, lse)
            w_a = jnp.exp2(lse_a - mx)
            w_b = jnp.exp2(lse - mx)
            tot = w_a + w_b
            o = (w_a * o_a + w_b * o) * (1.0 / tot)
            lse = mx + jnp.log2(tot)
        o_ref[hp, rows, :] = o
        if not is_last:
            lse_ref[hp, rows, :] = lse

    work = [(r, hp, j) for r in range(dil) for hp in range(W_DIL // LANES) for j in range(n_sub)]
    ahead, lag = BAND_SKEW[dil]
    _staged(work, scores, attend, emit, ahead, lag)


def _band_attn(cls, dil, batch, seq, prev, is_last):
    l_len = seq // dil
    tq = min(BAND_TILE[dil], l_len)
    sub = min(SUB_BAND, tq)
    n_l = l_len // tq
    halo_per_tile = tq // BAND_SIDE
    n_halo = l_len // BAND_SIDE
    width = dil * QKV_DIL

    cur = pl.BlockSpec((tq, width), lambda bi, l: (bi * n_l + l, 0))
    before = pl.BlockSpec((BAND_SIDE, width),
                          lambda bi, l: (bi * n_halo + jnp.maximum(l * halo_per_tile - 1, 0), 0))
    after = pl.BlockSpec((BAND_SIDE, width),
                         lambda bi, l: (bi * n_halo + jnp.minimum((l + 1) * halo_per_tile, n_halo - 1), 0))
    n_pairs = W_DIL // LANES
    nat = pl.BlockSpec((n_pairs, dil * tq, LANES), lambda bi, l: (0, bi * n_l + l, 0))
    band = _band_bias(sub)
    in_specs = [cur, before, after, pl.BlockSpec(band.shape, lambda bi, l: (0, 0))]
    args = [cls, cls, cls, band]
    has_prev = prev is not None
    if has_prev:
        in_specs += [nat, nat]
        args += list(prev)
    o_shape = jax.ShapeDtypeStruct((n_pairs, batch * seq, LANES), jnp.float32)
    n_out = 1 if is_last else 2
    return pl.pallas_call(
        functools.partial(_band_kernel, dil=dil, tq=tq, sub=sub, n_l=n_l, has_prev=has_prev, is_last=is_last),
        grid=(batch, n_l),
        in_specs=in_specs,
        out_specs=[nat] * n_out,
        out_shape=[o_shape] * n_out,
        compiler_params=_cparams("parallel", "parallel"),
        name=f"band_attn_d{dil}",
    )(*args)


def _band_bias(sub):
    nk = sub + 2 * BAND_SIDE
    r = np.arange(2 * sub)[:, None] % sub
    c = np.arange(nk)[None, :]
    ok = (c - r >= 0) & (c - r <= 2 * BAND_SIDE)
    return jnp.asarray(np.where(ok, 0.0, NEG), jnp.float32)


def _na_kernel(q_ref, kc_ref, kp_ref, kn_ref, vc_ref, vp_ref, vn_ref, bias_ref, o_ref, kk_ref, vv_ref, *, n_rows):
    step = pl.program_id(1)
    halo = (NB_ROWS // 2) * GRID_W
    cur = NA_ROWS_STEP * GRID_W
    kk_ref[0:halo, :] = kp_ref[...]
    kk_ref[halo:halo + cur, :] = kc_ref[...]
    kk_ref[halo + cur:, :] = kn_ref[...]
    vv_ref[0:halo, :] = vp_ref[...]
    vv_ref[halo:halo + cur, :] = vc_ref[...]
    vv_ref[halo + cur:, :] = vn_ref[...]

    low = lax.broadcasted_iota(jnp.int32, (1, LANES), 1) < HEAD_DIM
    n_keys = NB_ROWS * GRID_W
    row_base = step * NA_ROWS_STEP

    work = []
    for i in range(NA_ROWS_STEP):
        r = row_base + i
        r0 = jnp.clip(r - NB_ROWS // 2, 0, n_rows - NB_ROWS)
        start = pl.multiple_of((r0 - row_base + NB_ROWS // 2) * GRID_W, GRID_W)
        work += [(hp, i * GRID_W, start, r - r0) for hp in range(W_NA // LANES)]

    def scores(hp, qrow, start, variant):
        cs = slice(hp * LANES, (hp + 1) * LANES)
        q2 = _pair_rows(q_ref[qrow:qrow + GRID_W, cs], low)
        keys = kk_ref[pl.ds(start, n_keys), cs]
        s = lax.dot_general(q2, keys, (((1,), (1,)), ((), ())), preferred_element_type=jnp.float32)
        return s + bias_ref[hp, variant]

    def attend(s, hp, qrow, start, variant):
        return _softmax_pv(s, vv_ref[pl.ds(start, n_keys), hp * LANES:(hp + 1) * LANES])

    def emit(res, hp, qrow, start, variant):
        o2, den, _ = res
        o2 = o2 * (1.0 / den)
        o_ref[qrow:qrow + GRID_W, hp * LANES:(hp + 1) * LANES] = jnp.where(low, o2[:GRID_W], o2[GRID_W:])

    _staged(work, scores, attend, emit, *NA_SKEW)


def _na_attn(proj, bias_tab):
    b, seq, width = proj.shape
    n_rows = seq // GRID_W
    cur = NA_ROWS_STEP * GRID_W
    halo = (NB_ROWS // 2) * GRID_W
    halo_per_step = cur // halo
    n_halo = seq // halo
    n_pairs = W_NA // LANES

    def cur_spec(which):
        return pl.BlockSpec((None, cur, W_NA), lambda bi, i: (bi, i, which))

    def before(which):
        return pl.BlockSpec((None, halo, W_NA), lambda bi, i: (bi, jnp.maximum(i * halo_per_step - 1, 0), which))

    def after(which):
        return pl.BlockSpec((None, halo, W_NA),
                            lambda bi, i: (bi, jnp.minimum((i + 1) * halo_per_step, n_halo - 1), which))

    return pl.pallas_call(
        functools.partial(_na_kernel, n_rows=n_rows),
        grid=(b, n_rows // NA_ROWS_STEP),
        in_specs=[cur_spec(3), cur_spec(4), before(4), after(4), cur_spec(5), before(5), after(5),
                  pl.BlockSpec((n_pairs, NB_ROWS, 2 * GRID_W, NB_ROWS * GRID_W), lambda bi, i: (0, 0, 0, 0))],
        out_specs=pl.BlockSpec((None, cur, W_NA), lambda bi, i: (bi, i, 0)),
        out_shape=jax.ShapeDtypeStruct((b, seq, W_NA), jnp.float32),
        scratch_shapes=[pltpu.VMEM((cur + 2 * halo, W_NA), MXU_DTYPE),
                        pltpu.VMEM((cur + 2 * halo, W_NA), MXU_DTYPE)],
        compiler_params=_cparams("parallel", "parallel"),
        name="na_attn",
    )(proj, proj, proj, proj, proj, proj, proj, bias_tab)


def _na_bias_table(rpb):
    c = np.arange(GRID_W)
    wstart = np.clip(c - NB_COLS // 2, 0, GRID_W - NB_COLS)
    kc = np.arange(GRID_W)
    valid = (kc[None, :] >= wstart[:, None]) & (kc[None, :] < wstart[:, None] + NB_COLS)
    dc = np.clip(kc[None, :] - c[:, None] + NB_COLS - 1, 0, 2 * NB_COLS - 2)
    pick = jnp.asarray(dc[None, :, :] == np.arange(2 * NB_COLS - 1)[:, None, None])
    by_col = jnp.sum(jnp.where(pick[None, None], rpb.astype(jnp.float32)[:, :, :, None, None], 0.0), axis=2)
    by_col = jnp.where(jnp.asarray(valid)[None, None], by_col * LOG2E, NEG)
    tab = jnp.stack([by_col[:, NB_ROWS - 1 - v:2 * NB_ROWS - 1 - v] for v in range(NB_ROWS)], axis=1)
    tab = tab.transpose(0, 1, 3, 2, 4)
    n_pairs = N_HEADS_NA // 2
    tab = tab.reshape(n_pairs, 2, NB_ROWS, GRID_W, NB_ROWS * GRID_W).transpose(0, 2, 1, 3, 4)
    return tab.reshape(n_pairs, NB_ROWS, 2 * GRID_W, NB_ROWS * GRID_W)


def _out_route_kernel(l_ref, yd_ref, yn_ref, xa_ref, xb_ref, gd_ref, gn_ref, w_ref, g_ref, wr_ref, b_ref, tri_ref,
                      oa_ref, ob_ref, h_ref, eid_ref, rank_ref, gate_ref, cnt_ref, carry_ref, *, n_first):
    del l_ref

    def norm(y, g):
        ms = jnp.mean(y * y, axis=-1, keepdims=True)
        return (y * lax.rsqrt(ms + EPS) * g).astype(MXU_DTYPE)

    nd = norm(jnp.concatenate([yd_ref[j] for j in range(yd_ref.shape[0])], axis=1), gd_ref[...])
    nn = norm(yn_ref[...], gn_ref[...])
    y = jnp.dot(nd, w_ref[0:W_DIL, :], preferred_element_type=jnp.float32)
    y = y + jnp.dot(nn, w_ref[W_DIL:, :], preferred_element_type=jnp.float32)

    def residual_and_route(x_ref, o_ref):
        x = x_ref[...] + y
        o_ref[...] = x
        _route_tile(x, g_ref, wr_ref, b_ref, tri_ref, h_ref, eid_ref, rank_ref, gate_ref, cnt_ref, carry_ref)

    _on_part(n_first, residual_and_route, (xa_ref, oa_ref), (xb_ref, ob_ref))


def _route_tile(x, g_ref, w_ref, b_ref, tri_ref, h_ref, eid_ref, rank_ref, gate_ref, cnt_ref, carry_ref):
    @pl.when(pl.program_id(0) == 0)
    def _():
        carry_ref[...] = jnp.zeros_like(carry_ref)

    ms = jnp.mean(x * x, axis=-1, keepdims=True)
    h = x * lax.rsqrt(ms + EPS) * g_ref[...]
    h_hi = h.astype(MXU_DTYPE)
    h_ref[...] = h_hi
    h_lo = (h - h_hi.astype(jnp.float32)).astype(MXU_DTYPE)
    both = jnp.dot(h_hi, w_ref[...], preferred_element_type=jnp.float32)
    logits = both[:, :LANES] + both[:, LANES:]
    logits = logits + jnp.dot(h_lo, w_ref[:, :LANES], preferred_element_type=jnp.float32) + b_ref[...]
    lt = logits.T
    tm = lt.shape[1]

    best = lt[N_EXPERTS:N_EXPERTS + 1]
    gsel = jnp.zeros_like(best)
    for g in range(1, N_GROUPS):
        cand = lt[N_EXPERTS + g:N_EXPERTS + g + 1]
        upd = cand > best
        gsel = jnp.where(upd, float(g), gsel)
        best = jnp.where(upd, cand, best)
    den = jnp.zeros_like(best)
    for g in range(N_GROUPS):
        den = den + jnp.exp(lt[N_EXPERTS + g:N_EXPERTS + g + 1] - best)
    g_gate = 1.0 / den

    e8 = EXPERTS_PER_GROUP
    sel = lt[0:e8]
    for g in range(1, N_GROUPS):
        sel = jnp.where(gsel == float(g), lt[g * e8:(g + 1) * e8], sel)
    row = lax.broadcasted_iota(jnp.int32, (e8, tm), 0).astype(jnp.float32)
    v1 = jnp.max(sel, axis=0, keepdims=True)
    i1 = jnp.min(jnp.where(sel == v1, row, float(e8)), axis=0, keepdims=True)
    rest = jnp.where(row == i1, -jnp.inf, sel)
    v2 = jnp.max(rest, axis=0, keepdims=True)
    i2 = jnp.min(jnp.where(rest == v2, row, float(e8)), axis=0, keepdims=True)
    e2 = jnp.exp(v2 - v1)
    inv = 1.0 / (1.0 + e2)
    gate_ref[0:1, :] = g_gate * inv
    gate_ref[1:2, :] = g_gate * (e2 * inv)
    eid1 = gsel * float(e8) + i1
    eid2 = gsel * float(e8) + i2
    eid_ref[0:1, :] = eid1.astype(jnp.int32)
    eid_ref[1:2, :] = eid2.astype(jnp.int32)

    erow = lax.broadcasted_iota(jnp.int32, (N_EXPERTS, tm), 0).astype(jnp.float32)
    hit1 = erow == eid1
    hit2 = erow == eid2
    oh = jnp.concatenate([hit1, hit2], axis=0).astype(jnp.float32)
    pref = jnp.dot(oh.astype(MXU_DTYPE), tri_ref[...], preferred_element_type=jnp.float32)
    tot = jnp.sum(oh, axis=1, keepdims=True)
    carry = carry_ref[:, 0:1]
    val1 = carry + pref[:N_EXPERTS]
    val2 = carry + tot[:N_EXPERTS] + pref[N_EXPERTS:]
    rank_ref[0:1, :] = jnp.sum(jnp.where(hit1, val1, 0.0), axis=0, keepdims=True).astype(jnp.int32)
    rank_ref[1:2, :] = jnp.sum(jnp.where(hit2, val2, 0.0), axis=0, keepdims=True).astype(jnp.int32)
    new_carry = carry_ref[...] + (tot[:N_EXPERTS] + tot[N_EXPERTS:])
    carry_ref[...] = new_carry
    cnt_ref[...] = new_carry


def _out_proj_route(y_d, y_n, xa, xb, lidx, g_d, g_n, w_out, ln_g, w_route, b_route, tri):
    d = xa.shape[1]
    t = xa.shape[0] + xb.shape[0]
    tm = tri.shape[0]
    n_first = xa.shape[0] // tm
    per_tile = lambda i, l: (i, 0)
    per_layer = lambda i, l: (l[0], 0, 0)
    per_token_cols = lambda i, l: (0, i)
    grid_spec = pltpu.PrefetchScalarGridSpec(
        num_scalar_prefetch=1,
        grid=(t // tm,),
        in_specs=[
            pl.BlockSpec((W_DIL // LANES, tm, LANES), lambda i, l: (0, i, 0)),
            pl.BlockSpec((tm, W_NA), per_tile),
            *_pair_specs(tm, d, n_first),
            pl.BlockSpec((None, 1, W_DIL), per_layer),
            pl.BlockSpec((None, 1, W_NA), per_layer),
            pl.BlockSpec((None, d, d), per_layer),
            pl.BlockSpec((None, 1, d), per_layer),
            pl.BlockSpec((None, d, 2 * LANES), per_layer),
            pl.BlockSpec((None, 1, LANES), per_layer),
            pl.BlockSpec((tm, tm), lambda i, l: (0, 0)),
        ],
        out_specs=[
            *_pair_specs(tm, d, n_first),
            pl.BlockSpec((tm, d), per_tile),
            pl.BlockSpec((TOP_K, tm), per_token_cols),
            pl.BlockSpec((TOP_K, tm), per_token_cols),
            pl.BlockSpec((TOP_K, tm), per_token_cols),
            pl.BlockSpec((N_EXPERTS, LANES), lambda i, l: (0, 0)),
        ],
        scratch_shapes=[pltpu.VMEM((N_EXPERTS, LANES), jnp.float32)],
    )
    return pl.pallas_call(
        functools.partial(_out_route_kernel, n_first=n_first),
        grid_spec=grid_spec,
        out_shape=[
            jax.ShapeDtypeStruct(xa.shape, jnp.float32),
            jax.ShapeDtypeStruct(xb.shape, jnp.float32),
            jax.ShapeDtypeStruct((t, d), MXU_DTYPE),
            jax.ShapeDtypeStruct((TOP_K, t), jnp.int32),
            jax.ShapeDtypeStruct((TOP_K, t), jnp.int32),
            jax.ShapeDtypeStruct((TOP_K, t), jnp.float32),
            jax.ShapeDtypeStruct((N_EXPERTS, LANES), jnp.float32),
        ],
        compiler_params=_cparams("arbitrary"),
        name="out_proj_route",
    )(lidx, y_d, y_n, xa, xb, g_d, g_n, w_out, ln_g, w_route, b_route, tri)


def _expert_kernel(l_ref, be_ref, nu_ref, x_ref, wg_ref, wu_ref, wd_ref, o_ref, cg_ref, cu_ref, cd_ref):
    del l_ref
    i = pl.program_id(0)

    @pl.when((i == 0) | (be_ref[i] != be_ref[jnp.maximum(i - 1, 0)]))
    def _():
        cg_ref[...] = wg_ref[...].astype(MXU_DTYPE)
        cu_ref[...] = wu_ref[...].astype(MXU_DTYPE)
        cd_ref[...] = wd_ref[...].astype(MXU_DTYPE)

    @pl.when(i < nu_ref[0])
    def _():
        rows = x_ref.shape[0] // EXPERT_ROW_GROUPS
        gu = []
        for n in range(EXPERT_ROW_GROUPS):
            x = x_ref[n * rows:(n + 1) * rows, :]
            gu.append((jnp.dot(x, cg_ref[...], preferred_element_type=jnp.float32),
                       jnp.dot(x, cu_ref[...], preferred_element_type=jnp.float32)))
        for n, (g, u) in enumerate(gu):
            a = (g * jax.nn.sigmoid(g) * u).astype(MXU_DTYPE)
            y = jnp.dot(a, cd_ref[...], preferred_element_type=jnp.float32)
            o_ref[n * rows:(n + 1) * rows, :] = y.astype(o_ref.dtype)

    @pl.when(i >= nu_ref[0])
    def _():
        o_ref[...] = jnp.zeros_like(o_ref)


def _experts(xs, lidx, block_e, n_used, w_gate, w_up, w_down):
    n_slots, d = xs.shape
    n_blocks = n_slots // MOE_BLOCK
    grid_spec = pltpu.PrefetchScalarGridSpec(
        num_scalar_prefetch=3,
        grid=(n_blocks,),
        in_specs=[
            pl.BlockSpec((MOE_BLOCK, d), lambda i, l, be, nu: (i, 0)),
            pl.BlockSpec((None, None, d, D_EXPERT), lambda i, l, be, nu: (l[0], be[i], 0, 0)),
            pl.BlockSpec((None, None, d, D_EXPERT), lambda i, l, be, nu: (l[0], be[i], 0, 0)),
            pl.BlockSpec((None, None, D_EXPERT, d), lambda i, l, be, nu: (l[0], be[i], 0, 0)),
        ],
        out_specs=pl.BlockSpec((MOE_BLOCK, d), lambda i, l, be, nu: (i, 0)),
        scratch_shapes=[pltpu.VMEM((d, D_EXPERT), MXU_DTYPE), pltpu.VMEM((d, D_EXPERT), MXU_DTYPE),
                        pltpu.VMEM((D_EXPERT, d), MXU_DTYPE)],
    )
    return pl.pallas_call(
        _expert_kernel,
        grid_spec=grid_spec,
        out_shape=jax.ShapeDtypeStruct((n_slots, d), MXU_DTYPE),
        compiler_params=_cparams("arbitrary"),
        name="experts",
    )(lidx, block_e, n_used, xs, w_gate, w_up, w_down)


def _combine_kernel(y0_ref, y1_ref, gate_ref, xa_ref, xb_ref, oa_ref, ob_ref, *, n_first):
    g0 = gate_ref[:, 0:1]
    g1 = gate_ref[:, 1:2]
    y = y0_ref[...].astype(jnp.float32) * g0 + y1_ref[...].astype(jnp.float32) * g1

    def add(x_ref, o_ref):
        o_ref[...] = x_ref[...] + y

    _on_part(n_first, add, (xa_ref, oa_ref), (xb_ref, ob_ref))


def _combine(y_0, y_1, gate_t, xa, xb, tm):
    t, d = y_0.shape
    n_first = xa.shape[0] // tm
    row = pl.BlockSpec((tm, d), lambda i: (i, 0))
    pair = _pair_specs(tm, d, n_first)
    return pl.pallas_call(
        functools.partial(_combine_kernel, n_first=n_first),
        grid=(t // tm,),
        in_specs=[row, row, pl.BlockSpec((tm, TOP_K), lambda i: (i, 0))] + pair,
        out_specs=pair,
        out_shape=[jax.ShapeDtypeStruct(xa.shape, jnp.float32), jax.ShapeDtypeStruct(xb.shape, jnp.float32)],
        compiler_params=_cparams("arbitrary"),
        name="combine",
    )(y_0, y_1, gate_t, xa, xb)


def _rope_table(seq):
    half = ROT_DIM // 2
    inv = ROPE_THETA ** (-jnp.arange(half, dtype=jnp.float32) / half)
    ang = jnp.arange(seq).astype(jnp.float32)[:, None] * inv[None, :]
    cos, sin = jnp.cos(ang), jnp.sin(ang)
    ones = jnp.ones((seq, HEAD_DIM - ROT_DIM), jnp.float32)
    zeros = jnp.zeros((seq, HEAD_DIM - ROT_DIM), jnp.float32)
    zh = jnp.zeros((seq, half), jnp.float32)
    c = jnp.concatenate([cos, cos, ones], axis=1)
    s_up = jnp.concatenate([-sin, zh, zeros], axis=1)
    s_dn = jnp.concatenate([zh, sin, zeros], axis=1)
    rep = LANES // HEAD_DIM
    return jnp.concatenate([jnp.tile(c, (1, rep)), jnp.tile(s_up, (1, rep)), jnp.tile(s_dn, (1, rep))], axis=1)


def _split_hi_lo(w):
    hi = w.astype(MXU_DTYPE)
    lo = (w - hi.astype(jnp.float32)).astype(MXU_DTYPE)
    return jnp.concatenate([hi, lo], axis=-1)


def _trunk(x_first, x_second, p, consts):
    seq, d = x_first.shape[1:]
    b = x_first.shape[0] + x_second.shape[0]
    t = b * seq
    n_assign = t * TOP_K
    n_blocks = -(-n_assign // MOE_BLOCK) + N_EXPERTS
    n_slots = n_blocks * MOE_BLOCK
    tok = jnp.broadcast_to(jnp.arange(t, dtype=jnp.int32)[None, :], (TOP_K, t))

    def layer(x_parts, l):
        xa, xb = x_parts
        lidx = l.reshape(1)
        views = _in_proj(xa, xb, lidx, p["ln_mix"], p["w_in"], p["head_gains"], consts["blockdiag"], consts["rope"],
                         seq)
        merged = None
        for n, dil in enumerate(BAND_ORDER):
            merged = _band_attn(views[DILATIONS.index(dil)], dil, b, seq, merged, n == len(BAND_ORDER) - 1)
        y_d = merged[0]
        bias_tab = _na_bias_table(lax.dynamic_index_in_dim(p["rpb_na"], l, 0, keepdims=False))
        y_n = _na_attn(views[0].reshape(b, seq, 3 * d), bias_tab).reshape(t, W_NA)
        xa, xb, h, eid, rank, gate, counts = _out_proj_route(
            y_d, y_n, xa, xb, lidx, p["out_norm_dil"], p["out_norm_na"], p["w_out"], p["ln_ffn"], p["w_route"],
            p["b_route"], consts["tri"])
        counts = counts[:, 0].astype(jnp.int32)
        padded = (counts + MOE_BLOCK - 1) // MOE_BLOCK * MOE_BLOCK
        pend = jnp.cumsum(padded)
        pstart = pend - padded
        experts = jnp.arange(N_EXPERTS, dtype=jnp.int32)[:, None, None]
        dest = jnp.sum(jnp.where(eid[None] == experts, pstart[:, None, None], 0), axis=0) + rank
        block_e = jnp.minimum(jnp.sum(jnp.arange(n_blocks)[:, None] * MOE_BLOCK >= pend[None, :], axis=-1),
                              N_EXPERTS - 1).astype(jnp.int32)
        n_used = (pend[-1:] // MOE_BLOCK).astype(jnp.int32)
        slot_tok = (jnp.arange(n_slots, dtype=jnp.int32) % t).at[dest.reshape(-1)].set(
            tok.reshape(-1), mode="promise_in_bounds", unique_indices=True)
        xs = h.at[slot_tok].get(mode="promise_in_bounds")
        yb = _experts(xs, lidx, block_e, n_used, p["w_gate"], p["w_up"], p["w_down"])
        y_0 = yb.at[dest[0]].get(mode="promise_in_bounds")
        y_1 = yb.at[dest[1]].get(mode="promise_in_bounds")
        return tuple(_combine(y_0, y_1, gate.T, xa, xb, consts["tri"].shape[0])), None

    parts = (x_first.reshape(-1, d), x_second.reshape(-1, d))
    (xa, xb), _ = lax.scan(layer, parts, jnp.arange(DEPTH, dtype=jnp.int32))
    return xa.reshape(x_first.shape), xb.reshape(x_second.shape)


def kernel(x_prompt, x_sample, ln_mix, w_in, q_norm_dil, k_norm_dil, q_norm_na, k_norm_na, rpb_na, out_norm_dil,
           out_norm_na, w_out, ln_ffn, w_router_group, b_router_group, w_router_expert, b_router_expert, w_gate,
           w_up, w_down):
    assert x_prompt.shape[1:] == x_sample.shape[1:]
    seq = x_prompt.shape[1]
    assert seq % (max(DILATIONS) * BAND_SIDE) == 0 and seq % (NA_ROWS_STEP * GRID_W) == 0
    depth = ln_mix.shape[0]
    heads_per_group = W_DIL // HEAD_DIM
    head_gains = jnp.stack([jnp.tile(g, (1, heads_per_group)) for g in (q_norm_dil, k_norm_dil, q_norm_na, k_norm_na)],
                           axis=1)
    w_exp = w_router_expert.transpose(0, 2, 1, 3).reshape(depth, D_MODEL, N_EXPERTS)
    w_route = jnp.concatenate([w_exp, w_router_group], axis=-1)
    w_route = jnp.pad(w_route, ((0, 0), (0, 0), (0, LANES - w_route.shape[-1])))
    b_route = jnp.concatenate([b_router_expert.reshape(depth, N_EXPERTS), b_router_group], axis=-1)
    b_route = jnp.pad(b_route, ((0, 0), (0, LANES - b_route.shape[-1])))[:, None, :]
    params = {
        "ln_mix": ln_mix[:, None, :],
        "w_in": w_in.astype(MXU_DTYPE),
        "head_gains": head_gains,
        "rpb_na": rpb_na,
        "out_norm_dil": out_norm_dil[:, None, :],
        "out_norm_na": out_norm_na[:, None, :],
        "w_out": w_out.astype(MXU_DTYPE),
        "ln_ffn": ln_ffn[:, None, :],
        "w_route": _split_hi_lo(w_route),
        "b_route": b_route,
        "w_gate": w_gate,
        "w_up": w_up,
        "w_down": w_down,
    }
    head_of = np.arange(MXU_TILE) // HEAD_DIM
    tm_route = min(TM_ROUTE, (x_prompt.shape[0] + x_sample.shape[0]) * seq)
    consts = {
        "blockdiag": jnp.asarray((head_of[:, None] == head_of[None, :]) / HEAD_DIM, MXU_DTYPE),
        "rope": _rope_table(seq),
        "tri": jnp.asarray(np.arange(tm_route)[:, None] < np.arange(tm_route)[None, :], MXU_DTYPE),
    }
    return _trunk(x_prompt, x_sample, params, consts)
```

```python
import functools

import numpy as np
import jax
import jax.numpy as jnp
from jax import lax
from jax.experimental import pallas as pl
from jax.experimental.pallas import tpu as pltpu

D_MODEL = 1024
DEPTH = 4
HEAD_DIM = 64
N_HEADS_DIL = 8
N_HEADS_NA = 8
W_DIL = N_HEADS_DIL * HEAD_DIM
W_NA = N_HEADS_NA * HEAD_DIM
QKV_DIL = 3 * W_DIL
DILATIONS = (1, 4, 16)
BAND_SIDE = 64
ROT_DIM = HEAD_DIM // 4
ROPE_THETA = 500000.0
GRID_W = 64
NB_ROWS = 8
NB_COLS = 16
N_GROUPS = 4
EXPERTS_PER_GROUP = 8
N_EXPERTS = N_GROUPS * EXPERTS_PER_GROUP
TOP_K = 2
D_EXPERT = D_MODEL // 2
EPS = 1e-6
NEG = -1e30

LANES = 128
MXU_TILE = 256
LOG2E = 1.4426950408889634
MXU_DTYPE = jnp.bfloat16
VMEM_LIMIT = 48 * 1024 * 1024

TM_PROJ = 512
BAND_TILE = {1: 512, 4: 256, 16: 128}
SUB_BAND = 128
BAND_SKEW = {1: (2, 1), 4: (3, 1), 16: (3, 1)}
BAND_ORDER = (16, 4, 1)
NA_ROWS_STEP = 16
NA_SKEW = (3, 1)
TM_ROUTE = 512
MOE_BLOCK = 512
EXPERT_ROW_GROUPS = 2


def _cparams(*sem):
    return pltpu.CompilerParams(dimension_semantics=sem, vmem_limit_bytes=VMEM_LIMIT)


def _pair_specs(tm, d, n_first):
    first = pl.BlockSpec((tm, d), lambda i, *_: (jnp.minimum(i, n_first - 1), 0))
    second = pl.BlockSpec((tm, d), lambda i, *_: (jnp.maximum(i - n_first, 0), 0))
    return [first, second]


def _on_part(n_first, body, first_refs, second_refs):
    @pl.when(pl.program_id(0) < n_first)
    def _():
        body(*first_refs)

    @pl.when(pl.program_id(0) >= n_first)
    def _():
        body(*second_refs)


def _in_proj_kernel(l_ref, xa_ref, xb_ref, *rest, n_first):
    del l_ref
    _on_part(n_first, lambda x_ref: _in_proj_tile(x_ref, *rest), (xa_ref,), (xb_ref,))


def _in_proj_tile(x_ref, g_ref, w_ref, hg_ref, bd_ref, rope_ref, o_ref, *rest):
    class_refs, stage_ref = rest[:-1], rest[-1]
    tm = x_ref.shape[0]
    x = x_ref[...]
    ms = jnp.mean(x * x, axis=-1, keepdims=True)
    h = (x * lax.rsqrt(ms + EPS) * g_ref[...]).astype(MXU_DTYPE)
    cos = rope_ref[:, 0:LANES]
    sin_up = rope_ref[:, LANES:2 * LANES]
    sin_dn = rope_ref[:, 2 * LANES:3 * LANES]
    half = ROT_DIM // 2

    def project(c):
        return jnp.dot(h, w_ref[:, c * W_DIL:(c + 1) * W_DIL], preferred_element_type=jnp.float32)

    def finish(c, acc):
        if c in (0, 1, 3, 4):
            gi = (0, 1, None, 2, 3)[c]
            sq = (acc * acc).astype(MXU_DTYPE)
            bw = bd_ref.shape[0]
            msh = jnp.concatenate(
                [jnp.dot(sq[:, j * bw:(j + 1) * bw], bd_ref[...], preferred_element_type=jnp.float32)
                 for j in range(W_DIL // bw)], axis=1)
            acc = acc * lax.rsqrt(msh + EPS) * hg_ref[gi:gi + 1, :]
        if c in (0, 1):
            parts = []
            for j in range(W_DIL // LANES):
                t = acc[:, j * LANES:(j + 1) * LANES]
                parts.append(t * cos + pltpu.roll(t, LANES - half, 1) * sin_up + pltpu.roll(t, half, 1) * sin_dn)
            acc = jnp.concatenate(parts, axis=1)
        if c in (0, 3):
            acc = acc * (HEAD_DIM ** -0.5 * LOG2E)
        o_ref[:, c * W_DIL:(c + 1) * W_DIL] = acc.astype(o_ref.dtype)
        if c < 3:
            for j in range(W_DIL // LANES):
                stage_ref[j] = acc[:, j * LANES:(j + 1) * LANES]
            for dil, cls_ref in zip(DILATIONS[1:], class_refs):
                for r in range(dil):
                    for j in range(W_DIL // LANES):
                        col = r * QKV_DIL + c * W_DIL + j * LANES
                        rows = stage_ref[j, pl.ds(r, tm // dil, stride=dil), :]
                        cls_ref[:, col:col + LANES] = rows.astype(cls_ref.dtype)

    for c in range(w_ref.shape[1] // W_DIL):
        finish(c, project(c))


def _in_proj(xa, xb, lidx, ln_g, w_in, head_gains, blockdiag, rope_tab, seq):
    d = xa.shape[1]
    t = xa.shape[0] + xb.shape[0]
    tm = min(TM_PROJ, seq)
    n_seq_blocks = seq // tm
    n_first = xa.shape[0] // tm
    class_dils = DILATIONS[1:]
    grid_spec = pltpu.PrefetchScalarGridSpec(
        num_scalar_prefetch=1,
        grid=(t // tm,),
        in_specs=_pair_specs(tm, d, n_first) + [
            pl.BlockSpec((None, 1, d), lambda i, l: (l[0], 0, 0)),
            pl.BlockSpec((None, d, 3 * d), lambda i, l: (l[0], 0, 0)),
            pl.BlockSpec((None, 4, W_DIL), lambda i, l: (l[0], 0, 0)),
            pl.BlockSpec((MXU_TILE, MXU_TILE), lambda i, l: (0, 0)),
            pl.BlockSpec((tm, 3 * LANES), lambda i, l: (i % n_seq_blocks, 0)),
        ],
        out_specs=[pl.BlockSpec((tm, 3 * d), lambda i, l: (i, 0))]
        + [pl.BlockSpec((tm // dil, dil * QKV_DIL), lambda i, l: (i, 0)) for dil in class_dils],
        scratch_shapes=[pltpu.VMEM((W_DIL // LANES, tm, LANES), jnp.float32)],
    )
    return pl.pallas_call(
        functools.partial(_in_proj_kernel, n_first=n_first),
        grid_spec=grid_spec,
        out_shape=[jax.ShapeDtypeStruct((t, 3 * d), MXU_DTYPE)]
        + [jax.ShapeDtypeStruct((t // dil, dil * QKV_DIL), MXU_DTYPE) for dil in class_dils],
        compiler_params=_cparams("parallel"),
        name="in_proj",
    )(lidx, xa, xb, ln_g, w_in, head_gains, blockdiag, rope_tab)


def _pair_rows(x, low_mask):
    zero = jnp.zeros_like(x)
    return jnp.concatenate([jnp.where(low_mask, x, zero), jnp.where(low_mask, zero, x)], axis=0)


def _softmax_pv(s, v):
    m = jnp.max(s, axis=-1, keepdims=True)
    p = jnp.exp2((s - m).astype(MXU_DTYPE))
    v_ones = jnp.concatenate([v, jnp.ones_like(v)], axis=1)
    res = jnp.dot(p, v_ones, preferred_element_type=jnp.float32)
    return res[:, :v.shape[1]], res[:, v.shape[1]:], m


def _staged(work, scores, attend, emit, ahead, lag):
    s_queue, a_queue = [], []
    for step in range(len(work) + ahead + lag):
        if step < len(work):
            s_queue.append(scores(*work[step]))
        if 0 <= step - ahead < len(work):
            a_queue.append(attend(s_queue.pop(0), *work[step - ahead]))
        if 0 <= step - ahead - lag < len(work):
            emit(a_queue.pop(0), *work[step - ahead - lag])


def _band_kernel(*refs, dil, tq, sub, n_l, has_prev, is_last):
    cur_ref, before_ref, after_ref, band_ref = refs[:4]
    pos = 4
    if has_prev:
        po_ref, pl_ref = refs[pos:pos + 2]
        pos += 2
    o_ref = refs[pos]
    pos += 1
    if not is_last:
        lse_ref = refs[pos]

    li = pl.program_id(1)
    side = BAND_SIDE
    nk = sub + 2 * side
    low = lax.broadcasted_iota(jnp.int32, (1, LANES), 1) < HEAD_DIM
    col = lax.broadcasted_iota(jnp.int32, (1, nk), 1)
    n_sub = tq // sub

    def window(j, c0):
        cs = slice(c0, c0 + LANES)
        lo, hi = j * sub - side, (j + 1) * sub + side
        parts = [before_ref[:, cs]] if lo < 0 else []
        parts.append(cur_ref[max(lo, 0):min(hi, tq), cs])
        if hi > tq:
            parts.append(after_ref[:, cs])
        return jnp.concatenate(parts, axis=0)

    def scores(r, hp, j):
        q0 = r * QKV_DIL + hp * LANES
        q2 = _pair_rows(cur_ref[j * sub:(j + 1) * sub, q0:q0 + LANES], low)
        s = lax.dot_general(q2, window(j, q0 + W_DIL), (((1,), (1,)), ((), ())), preferred_element_type=jnp.float32)
        s = s + band_ref[...]
        if j == 0:
            s = jnp.where((col >= side) | (li > 0), s, NEG)
        if j == n_sub - 1:
            s = jnp.where((col < nk - side) | (li < n_l - 1), s, NEG)
        return s

    def attend(s, r, hp, j):
        return _softmax_pv(s, window(j, r * QKV_DIL + 2 * W_DIL + hp * LANES))

    def emit(res, r, hp, j):
        o2, den, m = res
        o2 = o2 * (1.0 / den)
        lse2 = m + jnp.log2(den)
        o = jnp.where(low, o2[:sub], o2[sub:])
        lse = jnp.where(low, lse2[:sub], lse2[sub:])
        if dil == 1:
            rows = slice(j * sub, (j + 1) * sub)
        else:
            rows = pl.ds(dil * j * sub + r, sub, stride=dil)
        if has_prev:
            o_a = po_ref[hp, rows, :]
            lse_a = pl_ref[hp, rows, :]
            mx = jnp.maximum(lse_a, lse)
            w_a = jnp.exp2(lse_a - mx)
            w_b = jnp.exp2(lse - mx)
            tot = w_a + w_b
            o = (w_a * o_a + w_b * o) * (1.0 / tot)
            lse = mx + jnp.log2(tot)
        o_ref[hp, rows, :] = o
        if not is_last:
            lse_ref[hp, rows, :] = lse

    work = [(r, hp, j) for r in range(dil) for hp in range(W_DIL // LANES) for j in range(n_sub)]
    ahead, lag = BAND_SKEW[dil]
    _staged(work, scores, attend, emit, ahead, lag)


def _band_attn(cls, dil, batch, seq, prev, is_last):
    l_len = seq // dil
    tq = min(BAND_TILE[dil], l_len)
    sub = min(SUB_BAND, tq)
    n_l = l_len // tq
    halo_per_tile = tq // BAND_SIDE
    n_halo = l_len // BAND_SIDE
    width = dil * QKV_DIL

    cur = pl.BlockSpec((tq, width), lambda bi, l: (bi * n_l + l, 0))
    before = pl.BlockSpec((BAND_SIDE, width),
                          lambda bi, l: (bi * n_halo + jnp.maximum(l * halo_per_tile - 1, 0), 0))
    after = pl.BlockSpec((BAND_SIDE, width),
                         lambda bi, l: (bi * n_halo + jnp.minimum((l + 1) * halo_per_tile, n_halo - 1), 0))
    n_pairs = W_DIL // LANES
    nat = pl.BlockSpec((n_pairs, dil * tq, LANES), lambda bi, l: (0, bi * n_l + l, 0))
    band = _band_bias(sub)
    in_specs = [cur, before, after, pl.BlockSpec(band.shape, lambda bi, l: (0, 0))]
    args = [cls, cls, cls, band]
    has_prev = prev is not None
    if has_prev:
        in_specs += [nat, nat]
        args += list(prev)
    o_shape = jax.ShapeDtypeStruct((n_pairs, batch * seq, LANES), jnp.float32)
    n_out = 1 if is_last else 2
    return pl.pallas_call(
        functools.partial(_band_kernel, dil=dil, tq=tq, sub=sub, n_l=n_l, has_prev=has_prev, is_last=is_last),
        grid=(batch, n_l),
        in_specs=in_specs,
        out_specs=[nat] * n_out,
        out_shape=[o_shape] * n_out,
        compiler_params=_cparams("parallel", "parallel"),
        name=f"band_attn_d{dil}",
    )(*args)


def _band_bias(sub):
    nk = sub + 2 * BAND_SIDE
    r = np.arange(2 * sub)[:, None] % sub
    c = np.arange(nk)[None, :]
    ok = (c - r >= 0) & (c - r <= 2 * BAND_SIDE)
    return jnp.asarray(np.where(ok, 0.0, NEG), jnp.float32)


def _na_kernel(q_ref, kc_ref, kp_ref, kn_ref, vc_ref, vp_ref, vn_ref, bias_ref, o_ref, kk_ref, vv_ref, *, n_rows):
    step = pl.program_id(1)
    halo = (NB_ROWS // 2) * GRID_W
    cur = NA_ROWS_STEP * GRID_W
    kk_ref[0:halo, :] = kp_ref[...]
    kk_ref[halo:halo + cur, :] = kc_ref[...]
    kk_ref[halo + cur:, :] = kn_ref[...]
    vv_ref[0:halo, :] = vp_ref[...]
    vv_ref[halo:halo + cur, :] = vc_ref[...]
    vv_ref[halo + cur:, :] = vn_ref[...]

    low = lax.broadcasted_iota(jnp.int32, (1, LANES), 1) < HEAD_DIM
    n_keys = NB_ROWS * GRID_W
    row_base = step * NA_ROWS_STEP

    work = []
    for i in range(NA_ROWS_STEP):
        r = row_base + i
        r0 = jnp.clip(r - NB_ROWS // 2, 0, n_rows - NB_ROWS)
        start = pl.multiple_of((r0 - row_base + NB_ROWS // 2) * GRID_W, GRID_W)
        work += [(hp, i * GRID_W, start, r - r0) for hp in range(W_NA // LANES)]

    def scores(hp, qrow, start, variant):
        cs = slice(hp * LANES, (hp + 1) * LANES)
        q2 = _pair_rows(q_ref[qrow:qrow + GRID_W, cs], low)
        keys = kk_ref[pl.ds(start, n_keys), cs]
        s = lax.dot_general(q2, keys, (((1,), (1,)), ((), ())), preferred_element_type=jnp.float32)
        return s + bias_ref[hp, variant]

    def attend(s, hp, qrow, start, variant):
        return _softmax_pv(s, vv_ref[pl.ds(start, n_keys), hp * LANES:(hp + 1) * LANES])

    def emit(res, hp, qrow, start, variant):
        o2, den, _ = res
        o2 = o2 * (1.0 / den)
        o_ref[qrow:qrow + GRID_W, hp * LANES:(hp + 1) * LANES] = jnp.where(low, o2[:GRID_W], o2[GRID_W:])

    _staged(work, scores, attend, emit, *NA_SKEW)


def _na_attn(proj, bias_tab):
    b, seq, width = proj.shape
    n_rows = seq // GRID_W
    cur = NA_ROWS_STEP * GRID_W
    halo = (NB_ROWS // 2) * GRID_W
    halo_per_step = cur // halo
    n_halo = seq // halo
    n_pairs = W_NA // LANES

    def cur_spec(which):
        return pl.BlockSpec((None, cur, W_NA), lambda bi, i: (bi, i, which))

    def before(which):
        return pl.BlockSpec((None, halo, W_NA), lambda bi, i: (bi, jnp.maximum(i * halo_per_step - 1, 0), which))

    def after(which):
        return pl.BlockSpec((None, halo, W_NA),
                            lambda bi, i: (bi, jnp.minimum((i + 1) * halo_per_step, n_halo - 1), which))

    return pl.pallas_call(
        functools.partial(_na_kernel, n_rows=n_rows),
        grid=(b, n_rows // NA_ROWS_STEP),
        in_specs=[cur_spec(3), cur_spec(4), before(4), after(4), cur_spec(5), before(5), after(5),
                  pl.BlockSpec((n_pairs, NB_ROWS, 2 * GRID_W, NB_ROWS * GRID_W), lambda bi, i: (0, 0, 0, 0))],
        out_specs=pl.BlockSpec((None, cur, W_NA), lambda bi, i: (bi, i, 0)),
        out_shape=jax.ShapeDtypeStruct((b, seq, W_NA), jnp.float32),
        scratch_shapes=[pltpu.VMEM((cur + 2 * halo, W_NA), MXU_DTYPE),
                        pltpu.VMEM((cur + 2 * halo, W_NA), MXU_DTYPE)],
        compiler_params=_cparams("parallel", "parallel"),
        name="na_attn",
    )(proj, proj, proj, proj, proj, proj, proj, bias_tab)


def _na_bias_table(rpb):
    c = np.arange(GRID_W)
    wstart = np.clip(c - NB_COLS // 2, 0, GRID_W - NB_COLS)
    kc = np.arange(GRID_W)
    valid = (kc[None, :] >= wstart[:, None]) & (kc[None, :] < wstart[:, None] + NB_COLS)
    dc = np.clip(kc[None, :] - c[:, None] + NB_COLS - 1, 0, 2 * NB_COLS - 2)
    pick = jnp.asarray(dc[None, :, :] == np.arange(2 * NB_COLS - 1)[:, None, None])
    by_col = jnp.sum(jnp.where(pick[None, None], rpb.astype(jnp.float32)[:, :, :, None, None], 0.0), axis=2)
    by_col = jnp.where(jnp.asarray(valid)[None, None], by_col * LOG2E, NEG)
    tab = jnp.stack([by_col[:, NB_ROWS - 1 - v:2 * NB_ROWS - 1 - v] for v in range(NB_ROWS)], axis=1)
    tab = tab.transpose(0, 1, 3, 2, 4)
    n_pairs = N_HEADS_NA // 2
    tab = tab.reshape(n_pairs, 2, NB_ROWS, GRID_W, NB_ROWS * GRID_W).transpose(0, 2, 1, 3, 4)
    return tab.reshape(n_pairs, NB_ROWS, 2 * GRID_W, NB_ROWS * GRID_W)


def _out_route_kernel(l_ref, yd_ref, yn_ref, xa_ref, xb_ref, gd_ref, gn_ref, w_ref, g_ref, wr_ref, b_ref, tri_ref,
                      oa_ref, ob_ref, h_ref, eid_ref, rank_ref, gate_ref, cnt_ref, carry_ref, *, n_first):
    del l_ref

    def norm(y, g):
        ms = jnp.mean(y * y, axis=-1, keepdims=True)
        return (y * lax.rsqrt(ms + EPS) * g).astype(MXU_DTYPE)

    def project_and_route(x_ref, o_ref):
        nd = norm(jnp.concatenate([yd_ref[j] for j in range(yd_ref.shape[0])], axis=1), gd_ref[...])
        nn = norm(yn_ref[...], gn_ref[...])
        y = jnp.dot(nd, w_ref[0:W_DIL, :], preferred_element_type=jnp.float32)
        y = y + jnp.dot(nn, w_ref[W_DIL:, :], preferred_element_type=jnp.float32)
        x = x_ref[...] + y
        o_ref[...] = x
        _route_tile(x, g_ref, wr_ref, b_ref, tri_ref, h_ref, eid_ref, rank_ref, gate_ref, cnt_ref, carry_ref)

    _on_part(n_first, project_and_route, (xa_ref, oa_ref), (xb_ref, ob_ref))


def _route_tile(x, g_ref, w_ref, b_ref, tri_ref, h_ref, eid_ref, rank_ref, gate_ref, cnt_ref, carry_ref):
    @pl.when(pl.program_id(0) == 0)
    def _():
        carry_ref[...] = jnp.zeros_like(carry_ref)

    ms = jnp.mean(x * x, axis=-1, keepdims=True)
    h = x * lax.rsqrt(ms + EPS) * g_ref[...]
    h_hi = h.astype(MXU_DTYPE)
    h_ref[...] = h_hi
    h_lo = (h - h_hi.astype(jnp.float32)).astype(MXU_DTYPE)
    both = jnp.dot(h_hi, w_ref[...], preferred_element_type=jnp.float32)
    logits = both[:, :LANES] + both[:, LANES:]
    logits = logits + jnp.dot(h_lo, w_ref[:, :LANES], preferred_element_type=jnp.float32) + b_ref[...]
    lt = logits.T
    tm = lt.shape[1]

    best = lt[N_EXPERTS:N_EXPERTS + 1]
    gsel = jnp.zeros_like(best)
    for g in range(1, N_GROUPS):
        cand = lt[N_EXPERTS + g:N_EXPERTS + g + 1]
        upd = cand > best
        gsel = jnp.where(upd, float(g), gsel)
        best = jnp.where(upd, cand, best)
    den = jnp.zeros_like(best)
    for g in range(N_GROUPS):
        den = den + jnp.exp(lt[N_EXPERTS + g:N_EXPERTS + g + 1] - best)
    g_gate = 1.0 / den

    e8 = EXPERTS_PER_GROUP
    sel = lt[0:e8]
    for g in range(1, N_GROUPS):
        sel = jnp.where(gsel == float(g), lt[g * e8:(g + 1) * e8], sel)
    row = lax.broadcasted_iota(jnp.int32, (e8, tm), 0).astype(jnp.float32)
    v1 = jnp.max(sel, axis=0, keepdims=True)
    i1 = jnp.min(jnp.where(sel == v1, row, float(e8)), axis=0, keepdims=True)
    rest = jnp.where(row == i1, -jnp.inf, sel)
    v2 = jnp.max(rest, axis=0, keepdims=True)
    i2 = jnp.min(jnp.where(rest == v2, row, float(e8)), axis=0, keepdims=True)
    e2 = jnp.exp(v2 - v1)
    inv = 1.0 / (1.0 + e2)
    gates = jnp.concatenate([g_gate * inv, g_gate * (e2 * inv), jnp.zeros((LANES - TOP_K, tm), jnp.float32)], axis=0)
    gate_ref[...] = gates.T
    eid1 = gsel * float(e8) + i1
    eid2 = gsel * float(e8) + i2
    eid_ref[0:1, :] = eid1.astype(jnp.int32)
    eid_ref[1:2, :] = eid2.astype(jnp.int32)

    erow = lax.broadcasted_iota(jnp.int32, (N_EXPERTS, tm), 0).astype(jnp.float32)
    hit1 = erow == eid1
    hit2 = erow == eid2
    oh = jnp.concatenate([hit1, hit2], axis=0).astype(jnp.float32)
    pref = jnp.dot(oh.astype(MXU_DTYPE), tri_ref[...], preferred_element_type=jnp.float32)
    tot = jnp.sum(oh, axis=1, keepdims=True)
    carry = carry_ref[:, 0:1]
    val1 = carry + pref[:N_EXPERTS]
    val2 = carry + tot[:N_EXPERTS] + pref[N_EXPERTS:]
    rank_ref[0:1, :] = jnp.sum(jnp.where(hit1, val1, 0.0), axis=0, keepdims=True).astype(jnp.int32)
    rank_ref[1:2, :] = jnp.sum(jnp.where(hit2, val2, 0.0), axis=0, keepdims=True).astype(jnp.int32)
    new_carry = carry_ref[...] + (tot[:N_EXPERTS] + tot[N_EXPERTS:])
    carry_ref[...] = new_carry
    cnt_ref[...] = new_carry


def _out_proj_route(y_d, y_n, xa, xb, lidx, g_d, g_n, w_out, ln_g, w_route, b_route, tri):
    d = xa.shape[1]
    t = xa.shape[0] + xb.shape[0]
    tm = tri.shape[0]
    n_first = xa.shape[0] // tm
    per_tile = lambda i, l: (i, 0)
    per_layer = lambda i, l: (l[0], 0, 0)
    per_token_cols = lambda i, l: (0, i)
    grid_spec = pltpu.PrefetchScalarGridSpec(
        num_scalar_prefetch=1,
        grid=(t // tm,),
        in_specs=[
            pl.BlockSpec((W_DIL // LANES, tm, LANES), lambda i, l: (0, i, 0)),
            pl.BlockSpec((tm, W_NA), per_tile),
            *_pair_specs(tm, d, n_first),
            pl.BlockSpec((None, 1, W_DIL), per_layer),
            pl.BlockSpec((None, 1, W_NA), per_layer),
            pl.BlockSpec((None, d, d), per_layer),
            pl.BlockSpec((None, 1, d), per_layer),
            pl.BlockSpec((None, d, 2 * LANES), per_layer),
            pl.BlockSpec((None, 1, LANES), per_layer),
            pl.BlockSpec((tm, tm), lambda i, l: (0, 0)),
        ],
        out_specs=[
            *_pair_specs(tm, d, n_first),
            pl.BlockSpec((tm, d), per_tile),
            pl.BlockSpec((TOP_K, tm), per_token_cols),
            pl.BlockSpec((TOP_K, tm), per_token_cols),
            pl.BlockSpec((tm, LANES), per_tile),
            pl.BlockSpec((N_EXPERTS, LANES), lambda i, l: (0, 0)),
        ],
        scratch_shapes=[pltpu.VMEM((N_EXPERTS, LANES), jnp.float32)],
    )
    return pl.pallas_call(
        functools.partial(_out_route_kernel, n_first=n_first),
        grid_spec=grid_spec,
        out_shape=[
            jax.ShapeDtypeStruct(xa.shape, jnp.float32),
            jax.ShapeDtypeStruct(xb.shape, jnp.float32),
            jax.ShapeDtypeStruct((t, d), MXU_DTYPE),
            jax.ShapeDtypeStruct((TOP_K, t), jnp.int32),
            jax.ShapeDtypeStruct((TOP_K, t), jnp.int32),
            jax.ShapeDtypeStruct((t, LANES), jnp.float32),
            jax.ShapeDtypeStruct((N_EXPERTS, LANES), jnp.float32),
        ],
        compiler_params=_cparams("arbitrary"),
        name="out_proj_route",
    )(lidx, y_d, y_n, xa, xb, g_d, g_n, w_out, ln_g, w_route, b_route, tri)


def _expert_kernel(l_ref, be_ref, nu_ref, x_ref, wg_ref, wu_ref, wd_ref, o_ref, cg_ref, cu_ref, cd_ref):
    del l_ref
    i = pl.program_id(0)

    @pl.when((i == 0) | (be_ref[i] != be_ref[jnp.maximum(i - 1, 0)]))
    def _():
        cg_ref[...] = wg_ref[...].astype(MXU_DTYPE)
        cu_ref[...] = wu_ref[...].astype(MXU_DTYPE)
        cd_ref[...] = wd_ref[...].astype(MXU_DTYPE)

    @pl.when(i < nu_ref[0])
    def _():
        rows = x_ref.shape[0] // EXPERT_ROW_GROUPS
        gu = []
        for n in range(EXPERT_ROW_GROUPS):
            x = x_ref[n * rows:(n + 1) * rows, :]
            gu.append((jnp.dot(x, cg_ref[...], preferred_element_type=jnp.float32),
                       jnp.dot(x, cu_ref[...], preferred_element_type=jnp.float32)))
        for n, (g, u) in enumerate(gu):
            a = (g * jax.nn.sigmoid(g) * u).astype(MXU_DTYPE)
            y = jnp.dot(a, cd_ref[...], preferred_element_type=jnp.float32)
            o_ref[n * rows:(n + 1) * rows, :] = y.astype(o_ref.dtype)

    @pl.when(i >= nu_ref[0])
    def _():
        o_ref[...] = jnp.zeros_like(o_ref)


def _experts(xs, lidx, block_e, n_used, w_gate, w_up, w_down):
    n_slots, d = xs.shape
    n_blocks = n_slots // MOE_BLOCK
    grid_spec = pltpu.PrefetchScalarGridSpec(
        num_scalar_prefetch=3,
        grid=(n_blocks,),
        in_specs=[
            pl.BlockSpec((MOE_BLOCK, d), lambda i, l, be, nu: (i, 0)),
            pl.BlockSpec((None, None, d, D_EXPERT), lambda i, l, be, nu: (l[0], be[i], 0, 0)),
            pl.BlockSpec((None, None, d, D_EXPERT), lambda i, l, be, nu: (l[0], be[i], 0, 0)),
            pl.BlockSpec((None, None, D_EXPERT, d), lambda i, l, be, nu: (l[0], be[i], 0, 0)),
        ],
        out_specs=pl.BlockSpec((MOE_BLOCK, d), lambda i, l, be, nu: (i, 0)),
        scratch_shapes=[pltpu.VMEM((d, D_EXPERT), MXU_DTYPE), pltpu.VMEM((d, D_EXPERT), MXU_DTYPE),
                        pltpu.VMEM((D_EXPERT, d), MXU_DTYPE)],
    )
    return pl.pallas_call(
        _expert_kernel,
        grid_spec=grid_spec,
        out_shape=jax.ShapeDtypeStruct((n_slots, d), MXU_DTYPE),
        compiler_params=_cparams("arbitrary"),
        name="experts",
    )(lidx, block_e, n_used, xs, w_gate, w_up, w_down)


def _combine_kernel(y0_ref, y1_ref, gate_ref, xa_ref, xb_ref, oa_ref, ob_ref, *, n_first):
    g0 = gate_ref[:, 0:1]
    g1 = gate_ref[:, 1:2]
    y = y0_ref[...].astype(jnp.float32) * g0 + y1_ref[...].astype(jnp.float32) * g1

    def add(x_ref, o_ref):
        o_ref[...] = x_ref[...] + y

    _on_part(n_first, add, (xa_ref, oa_ref), (xb_ref, ob_ref))


def _combine(y_0, y_1, gate_t, xa, xb, tm):
    t, d = y_0.shape
    n_first = xa.shape[0] // tm
    row = pl.BlockSpec((tm, d), lambda i: (i, 0))
    pair = _pair_specs(tm, d, n_first)
    return pl.pallas_call(
        functools.partial(_combine_kernel, n_first=n_first),
        grid=(t // tm,),
        in_specs=[row, row, pl.BlockSpec((tm, LANES), lambda i: (i, 0))] + pair,
        out_specs=pair,
        out_shape=[jax.ShapeDtypeStruct(xa.shape, jnp.float32), jax.ShapeDtypeStruct(xb.shape, jnp.float32)],
        compiler_params=_cparams("arbitrary"),
        name="combine",
    )(y_0, y_1, gate_t, xa, xb)


def _rope_table(seq):
    half = ROT_DIM // 2
    inv = ROPE_THETA ** (-jnp.arange(half, dtype=jnp.float32) / half)
    ang = jnp.arange(seq).astype(jnp.float32)[:, None] * inv[None, :]
    cos, sin = jnp.cos(ang), jnp.sin(ang)
    ones = jnp.ones((seq, HEAD_DIM - ROT_DIM), jnp.float32)
    zeros = jnp.zeros((seq, HEAD_DIM - ROT_DIM), jnp.float32)
    zh = jnp.zeros((seq, half), jnp.float32)
    c = jnp.concatenate([cos, cos, ones], axis=1)
    s_up = jnp.concatenate([-sin, zh, zeros], axis=1)
    s_dn = jnp.concatenate([zh, sin, zeros], axis=1)
    rep = LANES // HEAD_DIM
    return jnp.concatenate([jnp.tile(c, (1, rep)), jnp.tile(s_up, (1, rep)), jnp.tile(s_dn, (1, rep))], axis=1)


def _split_hi_lo(w):
    hi = w.astype(MXU_DTYPE)
    lo = (w - hi.astype(jnp.float32)).astype(MXU_DTYPE)
    return jnp.concatenate([hi, lo], axis=-1)


def _trunk(x_first, x_second, p, consts):
    seq, d = x_first.shape[1:]
    b = x_first.shape[0] + x_second.shape[0]
    t = b * seq
    n_assign = t * TOP_K
    n_blocks = -(-n_assign // MOE_BLOCK) + N_EXPERTS
    n_slots = n_blocks * MOE_BLOCK
    tok = jnp.broadcast_to(jnp.arange(t, dtype=jnp.int32)[None, :], (TOP_K, t))

    def layer(x_parts, l):
        xa, xb = x_parts
        lidx = l.reshape(1)
        views = _in_proj(xa, xb, lidx, p["ln_mix"], p["w_in"], p["head_gains"], consts["blockdiag"], consts["rope"],
                         seq)
        merged = None
        for n, dil in enumerate(BAND_ORDER):
            merged = _band_attn(views[DILATIONS.index(dil)], dil, b, seq, merged, n == len(BAND_ORDER) - 1)
        y_d = merged[0]
        bias_tab = _na_bias_table(lax.dynamic_index_in_dim(p["rpb_na"], l, 0, keepdims=False))
        y_n = _na_attn(views[0].reshape(b, seq, 3 * d), bias_tab).reshape(t, W_NA)
        xa, xb, h, eid, rank, gate, counts = _out_proj_route(
            y_d, y_n, xa, xb, lidx, p["out_norm_dil"], p["out_norm_na"], p["w_out"], p["ln_ffn"], p["w_route"],
            p["b_route"], consts["tri"])
        counts = counts[:, 0].astype(jnp.int32)
        padded = (counts + MOE_BLOCK - 1) // MOE_BLOCK * MOE_BLOCK
        pend = jnp.cumsum(padded)
        pstart = pend - padded
        experts = jnp.arange(N_EXPERTS, dtype=jnp.int32)[:, None, None]
        dest = jnp.sum(jnp.where(eid[None] == experts, pstart[:, None, None], 0), axis=0) + rank
        block_e = jnp.minimum(jnp.sum(jnp.arange(n_blocks)[:, None] * MOE_BLOCK >= pend[None, :], axis=-1),
                              N_EXPERTS - 1).astype(jnp.int32)
        n_used = (pend[-1:] // MOE_BLOCK).astype(jnp.int32)
        slot_tok = (jnp.arange(n_slots, dtype=jnp.int32) % t).at[dest.reshape(-1)].set(
            tok.reshape(-1), mode="promise_in_bounds", unique_indices=True)
        xs = h.at[slot_tok].get(mode="promise_in_bounds")
        yb = _experts(xs, lidx, block_e, n_used, p["w_gate"], p["w_up"], p["w_down"])
        y_0 = yb.at[dest[0]].get(mode="promise_in_bounds")
        y_1 = yb.at[dest[1]].get(mode="promise_in_bounds")
        return tuple(_combine(y_0, y_1, gate, xa, xb, consts["tri"].shape[0])), None

    parts, _ = layer((x_first.reshape(-1, d), x_second.reshape(-1, d)), jnp.int32(0))
    (xa, xb), _ = lax.scan(layer, parts, jnp.arange(1, DEPTH, dtype=jnp.int32))
    return xa.reshape(x_first.shape), xb.reshape(x_second.shape)


def kernel(x_prompt, x_sample, ln_mix, w_in, q_norm_dil, k_norm_dil, q_norm_na, k_norm_na, rpb_na, out_norm_dil,
           out_norm_na, w_out, ln_ffn, w_router_group, b_router_group, w_router_expert, b_router_expert, w_gate,
           w_up, w_down):
    assert x_prompt.shape[1:] == x_sample.shape[1:]
    seq = x_prompt.shape[1]
    assert seq % (max(DILATIONS) * BAND_SIDE) == 0 and seq % (NA_ROWS_STEP * GRID_W) == 0
    depth = ln_mix.shape[0]
    heads_per_group = W_DIL // HEAD_DIM
    head_gains = jnp.stack([jnp.tile(g, (1, heads_per_group)) for g in (q_norm_dil, k_norm_dil, q_norm_na, k_norm_na)],
                           axis=1)
    w_exp = w_router_expert.transpose(0, 2, 1, 3).reshape(depth, D_MODEL, N_EXPERTS)
    w_route = jnp.concatenate([w_exp, w_router_group], axis=-1)
    w_route = jnp.pad(w_route, ((0, 0), (0, 0), (0, LANES - w_route.shape[-1])))
    b_route = jnp.concatenate([b_router_expert.reshape(depth, N_EXPERTS), b_router_group], axis=-1)
    b_route = jnp.pad(b_route, ((0, 0), (0, LANES - b_route.shape[-1])))[:, None, :]
    params = {
        "ln_mix": ln_mix[:, None, :],
        "w_in": w_in.astype(MXU_DTYPE),
        "head_gains": head_gains,
        "rpb_na": rpb_na,
        "out_norm_dil": out_norm_dil[:, None, :],
        "out_norm_na": out_norm_na[:, None, :],
        "w_out": w_out.astype(MXU_DTYPE),
        "ln_ffn": ln_ffn[:, None, :],
        "w_route": _split_hi_lo(w_route),
        "b_route": b_route,
        "w_gate": w_gate,
        "w_up": w_up,
        "w_down": w_down,
    }
    head_of = np.arange(MXU_TILE) // HEAD_DIM
    tm_route = min(TM_ROUTE, (x_prompt.shape[0] + x_sample.shape[0]) * seq)
    consts = {
        "blockdiag": jnp.asarray((head_of[:, None] == head_of[None, :]) / HEAD_DIM, MXU_DTYPE),
        "rope": _rope_table(seq),
        "tri": jnp.asarray(np.arange(tm_route)[:, None] < np.arange(tm_route)[None, :], MXU_DTYPE),
    }
    return _trunk(x_prompt, x_sample, params, consts)
```

```python
import functools

import numpy as np
import jax
import jax.numpy as jnp
from jax import lax
from jax.experimental import pallas as pl
from jax.experimental.pallas import tpu as pltpu

D_MODEL = 1024
DEPTH = 4
HEAD_DIM = 64
N_HEADS_DIL = 8
N_HEADS_NA = 8
W_DIL = N_HEADS_DIL * HEAD_DIM
W_NA = N_HEADS_NA * HEAD_DIM
QKV_DIL = 3 * W_DIL
DILATIONS = (1, 4, 16)
BAND_SIDE = 64
ROT_DIM = HEAD_DIM // 4
ROPE_THETA = 500000.0
GRID_W = 64
NB_ROWS = 8
NB_COLS = 16
N_GROUPS = 4
EXPERTS_PER_GROUP = 8
N_EXPERTS = N_GROUPS * EXPERTS_PER_GROUP
TOP_K = 2
D_EXPERT = D_MODEL // 2
EPS = 1e-6
NEG = -1e30

LANES = 128
MXU_TILE = 256
LOG2E = 1.4426950408889634
MXU_DTYPE = jnp.bfloat16
VMEM_LIMIT = 48 * 1024 * 1024

TM_PROJ = 512
BAND_TILE = {1: 512, 4: 256, 16: 128}
SUB_BAND = 128
BAND_SKEW = {1: (2, 1), 4: (3, 1), 16: (3, 1)}
BAND_ORDER = (16, 4, 1)
NA_ROWS_STEP = 16
NA_SKEW = (3, 1)
TM_ROUTE = 512
MOE_BLOCK = 512
EXPERT_ROW_GROUPS = 2


def _cparams(*sem):
    return pltpu.CompilerParams(dimension_semantics=sem, vmem_limit_bytes=VMEM_LIMIT)


def _pair_specs(tm, d, n_first):
    first = pl.BlockSpec((tm, d), lambda i, *_: (jnp.minimum(i, n_first - 1), 0))
    second = pl.BlockSpec((tm, d), lambda i, *_: (jnp.maximum(i - n_first, 0), 0))
    return [first, second]


def _on_part(n_first, body, first_refs, second_refs):
    @pl.when(pl.program_id(0) < n_first)
    def _():
        body(*first_refs)

    @pl.when(pl.program_id(0) >= n_first)
    def _():
        body(*second_refs)


def _in_proj_kernel(l_ref, xa_ref, xb_ref, *rest, n_first):
    del l_ref
    _on_part(n_first, lambda x_ref: _in_proj_tile(x_ref, *rest), (xa_ref,), (xb_ref,))


def _in_proj_tile(x_ref, g_ref, w_ref, hg_ref, bd_ref, rope_ref, o_ref, *rest):
    class_refs, stage_ref = rest[:-1], rest[-1]
    tm = x_ref.shape[0]
    x = x_ref[...]
    ms = jnp.mean(x * x, axis=-1, keepdims=True)
    h = (x * lax.rsqrt(ms + EPS) * g_ref[...]).astype(MXU_DTYPE)
    cos = rope_ref[:, 0:LANES]
    sin_up = rope_ref[:, LANES:2 * LANES]
    sin_dn = rope_ref[:, 2 * LANES:3 * LANES]
    half = ROT_DIM // 2

    def project(c):
        return jnp.dot(h, w_ref[:, c * W_DIL:(c + 1) * W_DIL], preferred_element_type=jnp.float32)

    def finish(c, acc):
        if c in (0, 1, 3, 4):
            gi = (0, 1, None, 2, 3)[c]
            sq = (acc * acc).astype(MXU_DTYPE)
            bw = bd_ref.shape[0]
            msh = jnp.concatenate(
                [jnp.dot(sq[:, j * bw:(j + 1) * bw], bd_ref[...], preferred_element_type=jnp.float32)
                 for j in range(W_DIL // bw)], axis=1)
            acc = acc * lax.rsqrt(msh + EPS) * hg_ref[gi:gi + 1, :]
        if c in (0, 1):
            parts = []
            for j in range(W_DIL // LANES):
                t = acc[:, j * LANES:(j + 1) * LANES]
                parts.append(t * cos + pltpu.roll(t, LANES - half, 1) * sin_up + pltpu.roll(t, half, 1) * sin_dn)
            acc = jnp.concatenate(parts, axis=1)
        if c in (0, 3):
            acc = acc * (HEAD_DIM ** -0.5 * LOG2E)
        o_ref[:, c * W_DIL:(c + 1) * W_DIL] = acc.astype(o_ref.dtype)
        if c < 3:
            for j in range(W_DIL // LANES):
                stage_ref[j] = acc[:, j * LANES:(j + 1) * LANES]
            for dil, cls_ref in zip(DILATIONS[1:], class_refs):
                for r in range(dil):
                    for j in range(W_DIL // LANES):
                        col = r * QKV_DIL + c * W_DIL + j * LANES
                        rows = stage_ref[j, pl.ds(r, tm // dil, stride=dil), :]
                        cls_ref[:, col:col + LANES] = rows.astype(cls_ref.dtype)

    for c in range(w_ref.shape[1] // W_DIL):
        finish(c, project(c))


def _in_proj(xa, xb, lidx, ln_g, w_in, head_gains, blockdiag, rope_tab, seq):
    d = xa.shape[1]
    t = xa.shape[0] + xb.shape[0]
    tm = min(TM_PROJ, seq)
    n_seq_blocks = seq // tm
    n_first = xa.shape[0] // tm
    class_dils = DILATIONS[1:]
    grid_spec = pltpu.PrefetchScalarGridSpec(
        num_scalar_prefetch=1,
        grid=(t // tm,),
        in_specs=_pair_specs(tm, d, n_first) + [
            pl.BlockSpec((None, 1, d), lambda i, l: (l[0], 0, 0)),
            pl.BlockSpec((None, d, 3 * d), lambda i, l: (l[0], 0, 0)),
            pl.BlockSpec((None, 4, W_DIL), lambda i, l: (l[0], 0, 0)),
            pl.BlockSpec((MXU_TILE, MXU_TILE), lambda i, l: (0, 0)),
            pl.BlockSpec((tm, 3 * LANES), lambda i, l: (i % n_seq_blocks, 0)),
        ],
        out_specs=[pl.BlockSpec((tm, 3 * d), lambda i, l: (i, 0))]
        + [pl.BlockSpec((tm // dil, dil * QKV_DIL), lambda i, l: (i, 0)) for dil in class_dils],
        scratch_shapes=[pltpu.VMEM((W_DIL // LANES, tm, LANES), jnp.float32)],
    )
    return pl.pallas_call(
        functools.partial(_in_proj_kernel, n_first=n_first),
        grid_spec=grid_spec,
        out_shape=[jax.ShapeDtypeStruct((t, 3 * d), MXU_DTYPE)]
        + [jax.ShapeDtypeStruct((t // dil, dil * QKV_DIL), MXU_DTYPE) for dil in class_dils],
        compiler_params=_cparams("parallel"),
        name="in_proj",
    )(lidx, xa, xb, ln_g, w_in, head_gains, blockdiag, rope_tab)


def _pair_rows(x, low_mask):
    zero = jnp.zeros_like(x)
    return jnp.concatenate([jnp.where(low_mask, x, zero), jnp.where(low_mask, zero, x)], axis=0)


def _softmax_pv(s, v):
    m = jnp.max(s, axis=-1, keepdims=True)
    p = jnp.exp2((s - m).astype(MXU_DTYPE))
    v_ones = jnp.concatenate([v, jnp.ones_like(v)], axis=1)
    res = jnp.dot(p, v_ones, preferred_element_type=jnp.float32)
    return res[:, :v.shape[1]], res[:, v.shape[1]:], m


def _staged(work, scores, attend, emit, ahead, lag):
    s_queue, a_queue = [], []
    for step in range(len(work) + ahead + lag):
        if step < len(work):
            s_queue.append(scores(*work[step]))
        if 0 <= step - ahead < len(work):
            a_queue.append(attend(s_queue.pop(0), *work[step - ahead]))
        if 0 <= step - ahead - lag < len(work):
            emit(a_queue.pop(0), *work[step - ahead - lag])


def _band_kernel(*refs, dil, tq, sub, n_l, has_prev, is_last):
    cur_ref, before_ref, after_ref, band_ref = refs[:4]
    pos = 4
    if has_prev:
        po_ref, pl_ref = refs[pos:pos + 2]
        pos += 2
    o_ref = refs[pos]
    pos += 1
    if not is_last:
        lse_ref = refs[pos]

    li = pl.program_id(1)
    side = BAND_SIDE
    nk = sub + 2 * side
    low = lax.broadcasted_iota(jnp.int32, (1, LANES), 1) < HEAD_DIM
    col = lax.broadcasted_iota(jnp.int32, (1, nk), 1)
    n_sub = tq // sub

    def window(j, c0):
        cs = slice(c0, c0 + LANES)
        lo, hi = j * sub - side, (j + 1) * sub + side
        parts = [before_ref[:, cs]] if lo < 0 else []
        parts.append(cur_ref[max(lo, 0):min(hi, tq), cs])
        if hi > tq:
            parts.append(after_ref[:, cs])
        return jnp.concatenate(parts, axis=0)

    def scores(r, hp, j):
        q0 = r * QKV_DIL + hp * LANES
        q2 = _pair_rows(cur_ref[j * sub:(j + 1) * sub, q0:q0 + LANES], low)
        s = lax.dot_general(q2, window(j, q0 + W_DIL), (((1,), (1,)), ((), ())), preferred_element_type=jnp.float32)
        s = s + band_ref[...]
        if j == 0:
            s = jnp.where((col >= side) | (li > 0), s, NEG)
        if j == n_sub - 1:
            s = jnp.where((col < nk - side) | (li < n_l - 1), s, NEG)
        return s

    def attend(s, r, hp, j):
        return _softmax_pv(s, window(j, r * QKV_DIL + 2 * W_DIL + hp * LANES))

    def emit(res, r, hp, j):
        o2, den, m = res
        o2 = o2 * (1.0 / den)
        lse2 = m + jnp.log2(den)
        o = jnp.where(low, o2[:sub], o2[sub:])
        lse = jnp.where(low, lse2[:sub], lse2[sub:])
        if dil == 1:
            rows = slice(j * sub, (j + 1) * sub)
        else:
            rows = pl.ds(dil * j * sub + r, sub, stride=dil)
        if has_prev:
            o_a = po_ref[hp, rows, :]
            lse_a = pl_ref[hp, rows, :]
            mx = jnp.maximum(lse_a, lse)
            w_a = jnp.exp2(lse_a - mx)
            w_b = jnp.exp2(lse - mx)
            tot = w_a + w_b
            o = (w_a * o_a + w_b * o) * (1.0 / tot)
            lse = mx + jnp.log2(tot)
        o_ref[hp, rows, :] = o
        if not is_last:
            lse_ref[hp, rows, :] = lse

    work = [(r, hp, j) for r in range(dil) for hp in range(W_DIL // LANES) for j in range(n_sub)]
    ahead, lag = BAND_SKEW[dil]
    _staged(work, scores, attend, emit, ahead, lag)


def _band_attn(cls, dil, batch, seq, prev, is_last):
    l_len = seq // dil
    tq = min(BAND_TILE[dil], l_len)
    sub = min(SUB_BAND, tq)
    n_l = l_len // tq
    halo_per_tile = tq // BAND_SIDE
    n_halo = l_len // BAND_SIDE
    width = dil * QKV_DIL

    cur = pl.BlockSpec((tq, width), lambda bi, l: (bi * n_l + l, 0))
    before = pl.BlockSpec((BAND_SIDE, width),
                          lambda bi, l: (bi * n_halo + jnp.maximum(l * halo_per_tile - 1, 0), 0))
    after = pl.BlockSpec((BAND_SIDE, width),
                         lambda bi, l: (bi * n_halo + jnp.minimum((l + 1) * halo_per_tile, n_halo - 1), 0))
    n_pairs = W_DIL // LANES
    nat = pl.BlockSpec((n_pairs, dil * tq, LANES), lambda bi, l: (0, bi * n_l + l, 0))
    band = _band_bias(sub)
    in_specs = [cur, before, after, pl.BlockSpec(band.shape, lambda bi, l: (0, 0))]
    args = [cls, cls, cls, band]
    has_prev = prev is not None
    if has_prev:
        in_specs += [nat, nat]
        args += list(prev)
    o_shape = jax.ShapeDtypeStruct((n_pairs, batch * seq, LANES), jnp.float32)
    n_out = 1 if is_last else 2
    return pl.pallas_call(
        functools.partial(_band_kernel, dil=dil, tq=tq, sub=sub, n_l=n_l, has_prev=has_prev, is_last=is_last),
        grid=(batch, n_l),
        in_specs=in_specs,
        out_specs=[nat] * n_out,
        out_shape=[o_shape] * n_out,
        compiler_params=_cparams("parallel", "parallel"),
        name=f"band_attn_d{dil}",
    )(*args)


def _band_bias(sub):
    nk = sub + 2 * BAND_SIDE
    r = np.arange(2 * sub)[:, None] % sub
    c = np.arange(nk)[None, :]
    ok = (c - r >= 0) & (c - r <= 2 * BAND_SIDE)
    return jnp.asarray(np.where(ok, 0.0, NEG), jnp.float32)


def _na_kernel(q_ref, kc_ref, kp_ref, kn_ref, vc_ref, vp_ref, vn_ref, bias_ref, o_ref, kk_ref, vv_ref, *, n_rows):
    step = pl.program_id(1)
    halo = (NB_ROWS // 2) * GRID_W
    cur = NA_ROWS_STEP * GRID_W
    kk_ref[0:halo, :] = kp_ref[...]
    kk_ref[halo:halo + cur, :] = kc_ref[...]
    kk_ref[halo + cur:, :] = kn_ref[...]
    vv_ref[0:halo, :] = vp_ref[...]
    vv_ref[halo:halo + cur, :] = vc_ref[...]
    vv_ref[halo + cur:, :] = vn_ref[...]

    low = lax.broadcasted_iota(jnp.int32, (1, LANES), 1) < HEAD_DIM
    n_keys = NB_ROWS * GRID_W
    row_base = step * NA_ROWS_STEP

    work = []
    for i in range(NA_ROWS_STEP):
        r = row_base + i
        r0 = jnp.clip(r - NB_ROWS // 2, 0, n_rows - NB_ROWS)
        start = pl.multiple_of((r0 - row_base + NB_ROWS // 2) * GRID_W, GRID_W)
        work += [(hp, i * GRID_W, start, r - r0) for hp in range(W_NA // LANES)]

    def scores(hp, qrow, start, variant):
        cs = slice(hp * LANES, (hp + 1) * LANES)
        q2 = _pair_rows(q_ref[qrow:qrow + GRID_W, cs], low)
        keys = kk_ref[pl.ds(start, n_keys), cs]
        s = lax.dot_general(q2, keys, (((1,), (1,)), ((), ())), preferred_element_type=jnp.float32)
        return s + bias_ref[hp, variant]

    def attend(s, hp, qrow, start, variant):
        return _softmax_pv(s, vv_ref[pl.ds(start, n_keys), hp * LANES:(hp + 1) * LANES])

    def emit(res, hp, qrow, start, variant):
        o2, den, _ = res
        o2 = o2 * (1.0 / den)
        o_ref[qrow:qrow + GRID_W, hp * LANES:(hp + 1) * LANES] = jnp.where(low, o2[:GRID_W], o2[GRID_W:])

    _staged(work, scores, attend, emit, *NA_SKEW)


def _na_attn(proj, bias_tab):
    b, seq, width = proj.shape
    n_rows = seq // GRID_W
    cur = NA_ROWS_STEP * GRID_W
    halo = (NB_ROWS // 2) * GRID_W
    halo_per_step = cur // halo
    n_halo = seq // halo
    n_pairs = W_NA // LANES

    def cur_spec(which):
        return pl.BlockSpec((None, cur, W_NA), lambda bi, i: (bi, i, which))

    def before(which):
        return pl.BlockSpec((None, halo, W_NA), lambda bi, i: (bi, jnp.maximum(i * halo_per_step - 1, 0), which))

    def after(which):
        return pl.BlockSpec((None, halo, W_NA),
                            lambda bi, i: (bi, jnp.minimum((i + 1) * halo_per_step, n_halo - 1), which))

    return pl.pallas_call(
        functools.partial(_na_kernel, n_rows=n_rows),
        grid=(b, n_rows // NA_ROWS_STEP),
        in_specs=[cur_spec(3), cur_spec(4), before(4), after(4), cur_spec(5), before(5), after(5),
                  pl.BlockSpec((n_pairs, NB_ROWS, 2 * GRID_W, NB_ROWS * GRID_W), lambda bi, i: (0, 0, 0, 0))],
        out_specs=pl.BlockSpec((None, cur, W_NA), lambda bi, i: (bi, i, 0)),
        out_shape=jax.ShapeDtypeStruct((b, seq, W_NA), jnp.float32),
        scratch_shapes=[pltpu.VMEM((cur + 2 * halo, W_NA), MXU_DTYPE),
                        pltpu.VMEM((cur + 2 * halo, W_NA), MXU_DTYPE)],
        compiler_params=_cparams("parallel", "parallel"),
        name="na_attn",
    )(proj, proj, proj, proj, proj, proj, proj, bias_tab)


def _na_bias_table(rpb):
    c = np.arange(GRID_W)
    wstart = np.clip(c - NB_COLS // 2, 0, GRID_W - NB_COLS)
    kc = np.arange(GRID_W)
    valid = (kc[None, :] >= wstart[:, None]) & (kc[None, :] < wstart[:, None] + NB_COLS)
    dc = np.clip(kc[None, :] - c[:, None] + NB_COLS - 1, 0, 2 * NB_COLS - 2)
    pick = jnp.asarray(dc[None, :, :] == np.arange(2 * NB_COLS - 1)[:, None, None])
    by_col = jnp.sum(jnp.where(pick[None, None], rpb.astype(jnp.float32)[:, :, :, None, None], 0.0), axis=2)
    by_col = jnp.where(jnp.asarray(valid)[None, None], by_col * LOG2E, NEG)
    tab = jnp.stack([by_col[:, NB_ROWS - 1 - v:2 * NB_ROWS - 1 - v] for v in range(NB_ROWS)], axis=1)
    tab = tab.transpose(0, 1, 3, 2, 4)
    n_pairs = N_HEADS_NA // 2
    tab = tab.reshape(n_pairs, 2, NB_ROWS, GRID_W, NB_ROWS * GRID_W).transpose(0, 2, 1, 3, 4)
    return tab.reshape(n_pairs, NB_ROWS, 2 * GRID_W, NB_ROWS * GRID_W)


def _out_route_kernel(l_ref, yd_ref, yn_ref, xa_ref, xb_ref, gd_ref, gn_ref, w_ref, g_ref, wr_ref, b_ref, tri_ref,
                      oa_ref, ob_ref, h_ref, eid_ref, rank_ref, gate_ref, cnt_ref, carry_ref, *, n_first):
    del l_ref

    def norm(y, g):
        ms = jnp.mean(y * y, axis=-1, keepdims=True)
        return (y * lax.rsqrt(ms + EPS) * g).astype(MXU_DTYPE)

    def project_and_route(x_ref, o_ref):
        nd = norm(jnp.concatenate([yd_ref[j] for j in range(yd_ref.shape[0])], axis=1), gd_ref[...])
        nn = norm(yn_ref[...], gn_ref[...])
        y = jnp.dot(nd, w_ref[0:W_DIL, :], preferred_element_type=jnp.float32)
        y = y + jnp.dot(nn, w_ref[W_DIL:, :], preferred_element_type=jnp.float32)
        x = x_ref[...] + y
        o_ref[...] = x
        _route_tile(x, g_ref, wr_ref, b_ref, tri_ref, h_ref, eid_ref, rank_ref, gate_ref, cnt_ref, carry_ref)

    _on_part(n_first, project_and_route, (xa_ref, oa_ref), (xb_ref, ob_ref))


def _route_tile(x, g_ref, w_ref, b_ref, tri_ref, h_ref, eid_ref, rank_ref, gate_ref, cnt_ref, carry_ref):
    @pl.when(pl.program_id(0) == 0)
    def _():
        carry_ref[...] = jnp.zeros_like(carry_ref)

    ms = jnp.mean(x * x, axis=-1, keepdims=True)
    h = x * lax.rsqrt(ms + EPS) * g_ref[...]
    h_hi = h.astype(MXU_DTYPE)
    h_ref[...] = h_hi
    h_lo = (h - h_hi.astype(jnp.float32)).astype(MXU_DTYPE)
    both = jnp.dot(h_hi, w_ref[...], preferred_element_type=jnp.float32)
    logits = both[:, :LANES] + both[:, LANES:]
    logits = logits + jnp.dot(h_lo, w_ref[:, :LANES], preferred_element_type=jnp.float32) + b_ref[...]
    lt = logits.T
    tm = lt.shape[1]

    best = lt[N_EXPERTS:N_EXPERTS + 1]
    gsel = jnp.zeros_like(best)
    for g in range(1, N_GROUPS):
        cand = lt[N_EXPERTS + g:N_EXPERTS + g + 1]
        upd = cand > best
        gsel = jnp.where(upd, float(g), gsel)
        best = jnp.where(upd, cand, best)
    den = jnp.zeros_like(best)
    for g in range(N_GROUPS):
        den = den + jnp.exp(lt[N_EXPERTS + g:N_EXPERTS + g + 1] - best)
    g_gate = 1.0 / den

    e8 = EXPERTS_PER_GROUP
    sel = lt[0:e8]
    for g in range(1, N_GROUPS):
        sel = jnp.where(gsel == float(g), lt[g * e8:(g + 1) * e8], sel)
    row = lax.broadcasted_iota(jnp.int32, (e8, tm), 0).astype(jnp.float32)
    v1 = jnp.max(sel, axis=0, keepdims=True)
    i1 = jnp.min(jnp.where(sel == v1, row, float(e8)), axis=0, keepdims=True)
    rest = jnp.where(row == i1, -jnp.inf, sel)
    v2 = jnp.max(rest, axis=0, keepdims=True)
    i2 = jnp.min(jnp.where(rest == v2, row, float(e8)), axis=0, keepdims=True)
    e2 = jnp.exp(v2 - v1)
    inv = 1.0 / (1.0 + e2)
    gates = jnp.concatenate([g_gate * inv, g_gate * (e2 * inv), jnp.zeros((LANES - TOP_K, tm), jnp.float32)], axis=0)
    gate_ref[...] = gates.T
    eid1 = gsel * float(e8) + i1
    eid2 = gsel * float(e8) + i2
    eid_ref[0:1, :] = eid1.astype(jnp.int32)
    eid_ref[1:2, :] = eid2.astype(jnp.int32)

    erow = lax.broadcasted_iota(jnp.int32, (N_EXPERTS, tm), 0).astype(jnp.float32)
    hit1 = erow == eid1
    hit2 = erow == eid2
    oh = jnp.concatenate([hit1, hit2], axis=0).astype(jnp.float32)
    pref = jnp.dot(oh.astype(MXU_DTYPE), tri_ref[...], preferred_element_type=jnp.float32)
    tot = jnp.sum(oh, axis=1, keepdims=True)
    carry = carry_ref[:, 0:1]
    val1 = carry + pref[:N_EXPERTS]
    val2 = carry + tot[:N_EXPERTS] + pref[N_EXPERTS:]
    rank_ref[0:1, :] = jnp.sum(jnp.where(hit1, val1, 0.0), axis=0, keepdims=True).astype(jnp.int32)
    rank_ref[1:2, :] = jnp.sum(jnp.where(hit2, val2, 0.0), axis=0, keepdims=True).astype(jnp.int32)
    new_carry = carry_ref[...] + (tot[:N_EXPERTS] + tot[N_EXPERTS:])
    carry_ref[...] = new_carry
    cnt_ref[...] = new_carry


def _out_proj_route(y_d, y_n, xa, xb, lidx, g_d, g_n, w_out, ln_g, w_route, b_route, tri):
    d = xa.shape[1]
    t = xa.shape[0] + xb.shape[0]
    tm = tri.shape[0]
    n_first = xa.shape[0] // tm
    per_tile = lambda i, l: (i, 0)
    per_layer = lambda i, l: (l[0], 0, 0)
    per_token_cols = lambda i, l: (0, i)
    grid_spec = pltpu.PrefetchScalarGridSpec(
        num_scalar_prefetch=1,
        grid=(t // tm,),
        in_specs=[
            pl.BlockSpec((W_DIL // LANES, tm, LANES), lambda i, l: (0, i, 0)),
            pl.BlockSpec((tm, W_NA), per_tile),
            *_pair_specs(tm, d, n_first),
            pl.BlockSpec((None, 1, W_DIL), per_layer),
            pl.BlockSpec((None, 1, W_NA), per_layer),
            pl.BlockSpec((None, d, d), per_layer),
            pl.BlockSpec((None, 1, d), per_layer),
            pl.BlockSpec((None, d, 2 * LANES), per_layer),
            pl.BlockSpec((None, 1, LANES), per_layer),
            pl.BlockSpec((tm, tm), lambda i, l: (0, 0)),
        ],
        out_specs=[
            *_pair_specs(tm, d, n_first),
            pl.BlockSpec((tm, d), per_tile),
            pl.BlockSpec((TOP_K, tm), per_token_cols),
            pl.BlockSpec((TOP_K, tm), per_token_cols),
            pl.BlockSpec((tm, LANES), per_tile),
            pl.BlockSpec((N_EXPERTS, LANES), lambda i, l: (0, 0)),
        ],
        scratch_shapes=[pltpu.VMEM((N_EXPERTS, LANES), jnp.float32)],
    )
    return pl.pallas_call(
        functools.partial(_out_route_kernel, n_first=n_first),
        grid_spec=grid_spec,
        out_shape=[
            jax.ShapeDtypeStruct(xa.shape, jnp.float32),
            jax.ShapeDtypeStruct(xb.shape, jnp.float32),
            jax.ShapeDtypeStruct((t, d), MXU_DTYPE),
            jax.ShapeDtypeStruct((TOP_K, t), jnp.int32),
            jax.ShapeDtypeStruct((TOP_K, t), jnp.int32),
            jax.ShapeDtypeStruct((t, LANES), jnp.float32),
            jax.ShapeDtypeStruct((N_EXPERTS, LANES), jnp.float32),
        ],
        compiler_params=_cparams("arbitrary"),
        name="out_proj_route",
    )(lidx, y_d, y_n, xa, xb, g_d, g_n, w_out, ln_g, w_route, b_route, tri)


def _expert_kernel(l_ref, be_ref, nu_ref, x_ref, wg_ref, wu_ref, wd_ref, o_ref, cg_ref, cu_ref, cd_ref):
    del l_ref
    i = pl.program_id(0)

    @pl.when((i == 0) | (be_ref[i] != be_ref[jnp.maximum(i - 1, 0)]))
    def _():
        cg_ref[...] = wg_ref[...].astype(MXU_DTYPE)
        cu_ref[...] = wu_ref[...].astype(MXU_DTYPE)
        cd_ref[...] = wd_ref[...].astype(MXU_DTYPE)

    @pl.when(i < nu_ref[0])
    def _():
        rows = x_ref.shape[0] // EXPERT_ROW_GROUPS
        gu = []
        for n in range(EXPERT_ROW_GROUPS):
            x = x_ref[n * rows:(n + 1) * rows, :]
            gu.append((jnp.dot(x, cg_ref[...], preferred_element_type=jnp.float32),
                       jnp.dot(x, cu_ref[...], preferred_element_type=jnp.float32)))
        for n, (g, u) in enumerate(gu):
            a = (g * jax.nn.sigmoid(g) * u).astype(MXU_DTYPE)
            y = jnp.dot(a, cd_ref[...], preferred_element_type=jnp.float32)
            o_ref[n * rows:(n + 1) * rows, :] = y.astype(o_ref.dtype)

    @pl.when(i >= nu_ref[0])
    def _():
        o_ref[...] = jnp.zeros_like(o_ref)


def _experts(xs, lidx, block_e, n_used, w_gate, w_up, w_down):
    n_slots, d = xs.shape
    n_blocks = n_slots // MOE_BLOCK
    grid_spec = pltpu.PrefetchScalarGridSpec(
        num_scalar_prefetch=3,
        grid=(n_blocks,),
        in_specs=[
            pl.BlockSpec((MOE_BLOCK, d), lambda i, l, be, nu: (i, 0)),
            pl.BlockSpec((None, None, d, D_EXPERT), lambda i, l, be, nu: (l[0], be[i], 0, 0)),
            pl.BlockSpec((None, None, d, D_EXPERT), lambda i, l, be, nu: (l[0], be[i], 0, 0)),
            pl.BlockSpec((None, None, D_EXPERT, d), lambda i, l, be, nu: (l[0], be[i], 0, 0)),
        ],
        out_specs=pl.BlockSpec((MOE_BLOCK, d), lambda i, l, be, nu: (i, 0)),
        scratch_shapes=[pltpu.VMEM((d, D_EXPERT), MXU_DTYPE), pltpu.VMEM((d, D_EXPERT), MXU_DTYPE),
                        pltpu.VMEM((D_EXPERT, d), MXU_DTYPE)],
    )
    return pl.pallas_call(
        _expert_kernel,
        grid_spec=grid_spec,
        out_shape=jax.ShapeDtypeStruct((n_slots, d), MXU_DTYPE),
        compiler_params=_cparams("arbitrary"),
        name="experts",
    )(lidx, block_e, n_used, xs, w_gate, w_up, w_down)


def _combine_kernel(y0_ref, y1_ref, gate_ref, xa_ref, xb_ref, oa_ref, ob_ref, *, n_first):
    g0 = gate_ref[:, 0:1]
    g1 = gate_ref[:, 1:2]
    y = y0_ref[...].astype(jnp.float32) * g0 + y1_ref[...].astype(jnp.float32) * g1

    def add(x_ref, o_ref):
        o_ref[...] = x_ref[...] + y

    _on_part(n_first, add, (xa_ref, oa_ref), (xb_ref, ob_ref))


def _combine(y_0, y_1, gate_t, xa, xb, tm):
    t, d = y_0.shape
    n_first = xa.shape[0] // tm
    row = pl.BlockSpec((tm, d), lambda i: (i, 0))
    pair = _pair_specs(tm, d, n_first)
    return pl.pallas_call(
        functools.partial(_combine_kernel, n_first=n_first),
        grid=(t // tm,),
        in_specs=[row, row, pl.BlockSpec((tm, LANES), lambda i: (i, 0))] + pair,
        out_specs=pair,
        out_shape=[jax.ShapeDtypeStruct(xa.shape, jnp.float32), jax.ShapeDtypeStruct(xb.shape, jnp.float32)],
        compiler_params=_cparams("arbitrary"),
        name="combine",
    )(y_0, y_1, gate_t, xa, xb)


def _rope_table(seq):
    half = ROT_DIM // 2
    inv = ROPE_THETA ** (-jnp.arange(half, dtype=jnp.float32) / half)
    ang = jnp.arange(seq).astype(jnp.float32)[:, None] * inv[None, :]
    cos, sin = jnp.cos(ang), jnp.sin(ang)
    ones = jnp.ones((seq, HEAD_DIM - ROT_DIM), jnp.float32)
    zeros = jnp.zeros((seq, HEAD_DIM - ROT_DIM), jnp.float32)
    zh = jnp.zeros((seq, half), jnp.float32)
    c = jnp.concatenate([cos, cos, ones], axis=1)
    s_up = jnp.concatenate([-sin, zh, zeros], axis=1)
    s_dn = jnp.concatenate([zh, sin, zeros], axis=1)
    rep = LANES // HEAD_DIM
    return jnp.concatenate([jnp.tile(c, (1, rep)), jnp.tile(s_up, (1, rep)), jnp.tile(s_dn, (1, rep))], axis=1)


def _split_hi_lo(w):
    hi = w.astype(MXU_DTYPE)
    lo = (w - hi.astype(jnp.float32)).astype(MXU_DTYPE)
    return jnp.concatenate([hi, lo], axis=-1)


def _trunk(x_first, x_second, p, consts):
    seq, d = x_first.shape[1:]
    b = x_first.shape[0] + x_second.shape[0]
    t = b * seq
    n_assign = t * TOP_K
    n_blocks = -(-n_assign // MOE_BLOCK) + N_EXPERTS
    n_slots = n_blocks * MOE_BLOCK
    tok = jnp.broadcast_to(jnp.arange(t, dtype=jnp.int32)[None, :], (TOP_K, t))

    def layer(x_parts, l):
        xa, xb = x_parts
        lidx = l.reshape(1)
        views = _in_proj(xa, xb, lidx, p["ln_mix"], p["w_in"], p["head_gains"], consts["blockdiag"], consts["rope"],
                         seq)
        merged = None
        for n, dil in enumerate(BAND_ORDER):
            merged = _band_attn(views[DILATIONS.index(dil)], dil, b, seq, merged, n == len(BAND_ORDER) - 1)
        y_d = merged[0]
        bias_tab = _na_bias_table(lax.dynamic_index_in_dim(p["rpb_na"], l, 0, keepdims=False))
        y_n = _na_attn(views[0].reshape(b, seq, 3 * d), bias_tab).reshape(t, W_NA)
        xa, xb, h, eid, rank, gate, counts = _out_proj_route(
            y_d, y_n, xa, xb, lidx, p["out_norm_dil"], p["out_norm_na"], p["w_out"], p["ln_ffn"], p["w_route"],
            p["b_route"], consts["tri"])
        counts = counts[:, 0].astype(jnp.int32)
        padded = (counts + MOE_BLOCK - 1) // MOE_BLOCK * MOE_BLOCK
        pend = jnp.cumsum(padded)
        pstart = pend - padded
        experts = jnp.arange(N_EXPERTS, dtype=jnp.int32)[:, None, None]
        dest = jnp.sum(jnp.where(eid[None] == experts, pstart[:, None, None], 0), axis=0) + rank
        block_e = jnp.minimum(jnp.sum(jnp.arange(n_blocks)[:, None] * MOE_BLOCK >= pend[None, :], axis=-1),
                              N_EXPERTS - 1).astype(jnp.int32)
        n_used = (pend[-1:] // MOE_BLOCK).astype(jnp.int32)
        slot = jnp.arange(n_slots, dtype=jnp.int32).reshape(n_blocks, MOE_BLOCK)
        occupied = slot - pstart[block_e][:, None] < counts[block_e][:, None]
        occupied = occupied & (jnp.arange(n_blocks, dtype=jnp.int32)[:, None] < n_used)
        keys = jnp.concatenate([dest.reshape(-1), jnp.where(occupied, n_slots, slot).reshape(-1)])
        vals = jnp.concatenate([tok.reshape(-1), (slot % t).reshape(-1)])
        slot_tok = lax.sort((keys, vals), num_keys=1, is_stable=False)[1][:n_slots]
        xs = h.at[slot_tok].get(mode="promise_in_bounds")
        yb = _experts(xs, lidx, block_e, n_used, p["w_gate"], p["w_up"], p["w_down"])
        y_0 = yb.at[dest[0]].get(mode="promise_in_bounds")
        y_1 = yb.at[dest[1]].get(mode="promise_in_bounds")
        return tuple(_combine(y_0, y_1, gate, xa, xb, consts["tri"].shape[0])), None

    parts, _ = layer((x_first.reshape(-1, d), x_second.reshape(-1, d)), jnp.int32(0))
    (xa, xb), _ = lax.scan(layer, parts, jnp.arange(1, DEPTH, dtype=jnp.int32))
    return xa.reshape(x_first.shape), xb.reshape(x_second.shape)


def kernel(x_prompt, x_sample, ln_mix, w_in, q_norm_dil, k_norm_dil, q_norm_na, k_norm_na, rpb_na, out_norm_dil,
           out_norm_na, w_out, ln_ffn, w_router_group, b_router_group, w_router_expert, b_router_expert, w_gate,
           w_up, w_down):
    assert x_prompt.shape[1:] == x_sample.shape[1:]
    seq = x_prompt.shape[1]
    assert seq % (max(DILATIONS) * BAND_SIDE) == 0 and seq % (NA_ROWS_STEP * GRID_W) == 0
    depth = ln_mix.shape[0]
    heads_per_group = W_DIL // HEAD_DIM
    head_gains = jnp.stack([jnp.tile(g, (1, heads_per_group)) for g in (q_norm_dil, k_norm_dil, q_norm_na, k_norm_na)],
                           axis=1)
    w_exp = w_router_expert.transpose(0, 2, 1, 3).reshape(depth, D_MODEL, N_EXPERTS)
    w_route = jnp.concatenate([w_exp, w_router_group], axis=-1)
    w_route = jnp.pad(w_route, ((0, 0), (0, 0), (0, LANES - w_route.shape[-1])))
    b_route = jnp.concatenate([b_router_expert.reshape(depth, N_EXPERTS), b_router_group], axis=-1)
    b_route = jnp.pad(b_route, ((0, 0), (0, LANES - b_route.shape[-1])))[:, None, :]
    params = {
        "ln_mix": ln_mix[:, None, :],
        "w_in": w_in.astype(MXU_DTYPE),
        "head_gains": head_gains,
        "rpb_na": rpb_na,
        "out_norm_dil": out_norm_dil[:, None, :],
        "out_norm_na": out_norm_na[:, None, :],
        "w_out": w_out.astype(MXU_DTYPE),
        "ln_ffn": ln_ffn[:, None, :],
        "w_route": _split_hi_lo(w_route),
        "b_route": b_route,
        "w_gate": w_gate,
        "w_up": w_up,
        "w_down": w_down,
    }
    head_of = np.arange(MXU_TILE) // HEAD_DIM
    tm_route = min(TM_ROUTE, (x_prompt.shape[0] + x_sample.shape[0]) * seq)
    consts = {
        "blockdiag": jnp.asarray((head_of[:, None] == head_of[None, :]) / HEAD_DIM, MXU_DTYPE),
        "rope": _rope_table(seq),
        "tri": jnp.asarray(np.arange(tm_route)[:, None] < np.arange(tm_route)[None, :], MXU_DTYPE),
    }
    return _trunk(x_prompt, x_sample, params, consts)
```
